```python
import math
import jax, jax.numpy as jnp
from jax import lax
import numpy as np

D_MODEL = 1024
BATCH = 32
SEQ = 2048
DEPTH = 1

D_MIX = D_MODEL
GDN_WIDTH = D_MIX // 2
GDN_HEAD_DIM = 128
GDN_HEADS = GDN_WIDTH // GDN_HEAD_DIM
GDN_CHUNK = 64
LRU_WIDTH = D_MIX - GDN_WIDTH
LRU_BLOCKS = 8
LRU_BLOCK_DIM = LRU_WIDTH // LRU_BLOCKS
LRU_C = 8.0
CONV_WIDTH = 4
D_FF = 4 * D_MODEL
EPS = 1e-6
SPLIT_SIZES = (3 * GDN_WIDTH, GDN_WIDTH, GDN_HEADS, GDN_HEADS, LRU_WIDTH, LRU_WIDTH)
IN_COLS = sum(SPLIT_SIZES)
SPLIT_IDX = tuple(int(i) for i in np.cumsum(SPLIT_SIZES)[:-1])

kernel_name = "hymba_gdn_rglru_sqrelu_block"


def rmsnorm(x, w):
    xf = x.astype(jnp.float32)
    y = xf * lax.rsqrt(jnp.mean(xf * xf, axis=-1, keepdims=True) + EPS)
    return (y * w.astype(jnp.float32)).astype(x.dtype)


def l2norm(x):
    xf = x.astype(jnp.float32)
    return xf * lax.rsqrt(jnp.sum(xf * xf, axis=-1, keepdims=True) + EPS)


def causal_depthwise_conv(x, w):
    c = x.shape[-1]
    return lax.conv_general_dilated(
        x, w[:, None, :].astype(x.dtype), window_strides=(1,),
        padding=[(CONV_WIDTH - 1, 0)], dimension_numbers=("NWC", "WIO", "NWC"),
        feature_group_count=c)


def gated_delta_rule_chunked(q, k, v, g, beta):
    f32 = jnp.float32
    b, s, h, dk = q.shape
    dv = v.shape[-1]
    c = GDN_CHUNK
    n = s // c

    def to_chunks(t):
        t = jnp.moveaxis(t.astype(f32), 2, 1)
        return t.reshape(b, h, n, c, *t.shape[3:])

    q = to_chunks(q) * (dk ** -0.5)
    k = to_chunks(k)
    v = to_chunks(v)
    g_cum = jnp.cumsum(to_chunks(g), axis=-1)
    beta = to_chunks(beta)

    causal = jnp.tril(jnp.ones((c, c), dtype=bool))
    strict = jnp.tril(jnp.ones((c, c), dtype=bool), -1)
    diff = g_cum[..., :, None] - g_cum[..., None, :]
    decay = jnp.where(causal, jnp.exp(jnp.where(causal, diff, 0.0)), 0.0)

    k_beta = k * beta[..., None]
    v_beta = v * beta[..., None]
    a_low = jnp.where(strict, jnp.einsum("bhnid,bhnjd->bhnij", k_beta, k) * decay, 0.0)
    i_plus_a = a_low + jnp.eye(c, dtype=f32)
    rhs = jnp.concatenate([v_beta, k_beta * jnp.exp(g_cum)[..., None]], axis=-1)
    sol = lax.linalg.triangular_solve(i_plus_a, rhs, left_side=True, lower=True)
    u = sol[..., :dv]
    w = sol[..., dv:]
    qk_intra = jnp.where(causal, jnp.einsum("bhnid,bhnjd->bhnij", q, k) * decay, 0.0)

    def step(state, inp):
        q_c, k_c, u_c, w_c, attn_c, g_c = inp
        v_new = u_c - jnp.einsum("bhck,bhkv->bhcv", w_c, state)
        o_c = (jnp.einsum("bhck,bhkv->bhcv", q_c * jnp.exp(g_c)[..., None], state)
               + jnp.einsum("bhij,bhjv->bhiv", attn_c, v_new))
        g_last = g_c[..., -1]
        k_dec = k_c * jnp.exp(g_last[..., None] - g_c)[..., None]
        state = state * jnp.exp(g_last)[..., None, None] + jnp.einsum("bhck,bhcv->bhkv", k_dec, v_new)
        return state, o_c

    xs = tuple(jnp.moveaxis(t, 2, 0) for t in (q, k, u, w, qk_intra, g_cum))
    state0 = jnp.zeros((b, h, dk, dv), f32)
    _, o = lax.scan(step, state0, xs)
    o = jnp.moveaxis(o, 0, 2).reshape(b, h, s, dv)
    return jnp.moveaxis(o, 1, 2)


def rg_lru(x, wa, ba, wx, bx, a_param):
    f32 = jnp.float32
    b, s, d = x.shape
    xf = x.astype(f32)
    xb = xf.reshape(b, s, LRU_BLOCKS, LRU_BLOCK_DIM)
    r = jax.nn.sigmoid(jnp.einsum("bsgi,gij->bsgj", xb, wa.astype(f32)) + ba.astype(f32)).reshape(b, s, d)
    i = jax.nn.sigmoid(jnp.einsum("bsgi,gij->bsgj", xb, wx.astype(f32)) + bx.astype(f32)).reshape(b, s, d)
    log_a = -LRU_C * r * jax.nn.softplus(-a_param.astype(f32))
    a = jnp.exp(log_a)
    gated_x = jnp.sqrt(-jnp.expm1(2.0 * log_a)) * (i * xf)

    def combine(c1, c2):
        a1, b1 = c1
        a2, b2 = c2
        return a1 * a2, a2 * b1 + b2

    _, h = lax.associative_scan(combine, (a, gated_x), axis=1)
    return h


def hybrid_layer(x, norm_mix_w, w_in, gdn_conv_w, gdn_A_log, gdn_dt_bias, gdn_norm_w,
                 lru_conv_w, lru_conv_b, lru_gate_a_w, lru_gate_a_b, lru_gate_x_w, lru_gate_x_b,
                 lru_a_param, w_out, norm_mlp_w, w_ff1, w_ff2):
    b, s, _ = x.shape
    dt = x.dtype
    h = rmsnorm(x, norm_mix_w)
    proj = jnp.einsum("bsd,de->bse", h, w_in.astype(dt))
    qkv, z, b_logit, a_logit, lru_x, lru_gate = jnp.split(proj, SPLIT_IDX, axis=-1)

    qkv = jax.nn.silu(causal_depthwise_conv(qkv, gdn_conv_w))
    q, k, v = jnp.split(qkv, 3, axis=-1)
    q = l2norm(q.reshape(b, s, GDN_HEADS, GDN_HEAD_DIM))
    k = l2norm(k.reshape(b, s, GDN_HEADS, GDN_HEAD_DIM))
    v = v.reshape(b, s, GDN_HEADS, GDN_HEAD_DIM)
    beta = jax.nn.sigmoid(b_logit.astype(jnp.float32))
    g = -jnp.exp(gdn_A_log.astype(jnp.float32)) * jax.nn.softplus(
        a_logit.astype(jnp.float32) + gdn_dt_bias.astype(jnp.float32))
    o = gated_delta_rule_chunked(q, k, v, g, beta)
    z = z.reshape(b, s, GDN_HEADS, GDN_HEAD_DIM).astype(jnp.float32)
    o = rmsnorm(o, gdn_norm_w) * jax.nn.silu(z)
    gdn_out = o.reshape(b, s, GDN_WIDTH).astype(dt)

    xr = causal_depthwise_conv(lru_x, lru_conv_w) + lru_conv_b.astype(dt)
    hr = rg_lru(xr, lru_gate_a_w, lru_gate_a_b, lru_gate_x_w, lru_gate_x_b, lru_a_param)
    lru_out = (hr * jax.nn.gelu(lru_gate.astype(jnp.float32))).astype(dt)

    mix = jnp.concatenate([gdn_out, lru_out], axis=-1)
    x = x + jnp.einsum("bse,ed->bsd", mix, w_out.astype(dt))

    m = rmsnorm(x, norm_mlp_w)
    u = jax.nn.relu(jnp.einsum("bsd,df->bsf", m, w_ff1.astype(dt)))
    x = x + jnp.einsum("bsf,fd->bsd", u * u, w_ff2.astype(dt))
    return x


def setup_inputs(seed: int = 0) -> dict:
    key = jax.random.key(seed)
    ks = jax.random.split(key, 24)
    f32 = jnp.float32
    nrm = lambda k, shape, scale: (jax.random.normal(k, shape, f32) * scale)
    x = jax.random.normal(ks[0], (BATCH, SEQ, D_MODEL), f32)
    norm_mix_w = 1.0 + nrm(ks[1], (DEPTH, D_MODEL), 0.02)
    w_in = nrm(ks[2], (DEPTH, D_MODEL, IN_COLS), D_MODEL ** -0.5)
    gdn_conv_w = nrm(ks[3], (DEPTH, CONV_WIDTH, 3 * GDN_WIDTH), CONV_WIDTH ** -0.5)
    gdn_A_log = jnp.log(jax.random.uniform(ks[4], (DEPTH, GDN_HEADS), f32, 1.0, 16.0))
    dt0 = jnp.exp(jax.random.uniform(ks[5], (DEPTH, GDN_HEADS), f32, math.log(1e-3), math.log(1e-1)))
    gdn_dt_bias = dt0 + jnp.log(-jnp.expm1(-dt0))
    gdn_norm_w = 1.0 + nrm(ks[6], (DEPTH, GDN_HEAD_DIM), 0.02)
    lru_conv_w = nrm(ks[7], (DEPTH, CONV_WIDTH, LRU_WIDTH), CONV_WIDTH ** -0.5)
    lru_conv_b = nrm(ks[8], (DEPTH, LRU_WIDTH), 0.01)
    lru_gate_a_w = nrm(ks[9], (DEPTH, LRU_BLOCKS, LRU_BLOCK_DIM, LRU_BLOCK_DIM), LRU_BLOCK_DIM ** -0.5)
    lru_gate_a_b = nrm(ks[10], (DEPTH, LRU_BLOCKS, LRU_BLOCK_DIM), 0.01)
    lru_gate_x_w = nrm(ks[11], (DEPTH, LRU_BLOCKS, LRU_BLOCK_DIM, LRU_BLOCK_DIM), LRU_BLOCK_DIM ** -0.5)
    lru_gate_x_b = nrm(ks[12], (DEPTH, LRU_BLOCKS, LRU_BLOCK_DIM), 0.01)
    a_c = jax.random.uniform(ks[13], (DEPTH, LRU_WIDTH), f32, 0.9, 0.999)
    sig_l = a_c ** (1.0 / LRU_C)
    lru_a_param = jnp.log(sig_l) - jnp.log1p(-sig_l)
    w_out = nrm(ks[14], (DEPTH, D_MIX, D_MODEL), D_MIX ** -0.5)
    norm_mlp_w = 1.0 + nrm(ks[15], (DEPTH, D_MODEL), 0.02)
    w_ff1 = nrm(ks[16], (DEPTH, D_MODEL, D_FF), D_MODEL ** -0.5)
    w_ff2 = nrm(ks[17], (DEPTH, D_FF, D_MODEL), D_FF ** -0.5)
    final_norm_w = 1.0 + nrm(ks[18], (D_MODEL,), 0.02)
    return {"x": x, "norm_mix_w": norm_mix_w, "w_in": w_in, "gdn_conv_w": gdn_conv_w,
            "gdn_A_log": gdn_A_log, "gdn_dt_bias": gdn_dt_bias, "gdn_norm_w": gdn_norm_w,
            "lru_conv_w": lru_conv_w, "lru_conv_b": lru_conv_b,
            "lru_gate_a_w": lru_gate_a_w, "lru_gate_a_b": lru_gate_a_b,
            "lru_gate_x_w": lru_gate_x_w, "lru_gate_x_b": lru_gate_x_b,
            "lru_a_param": lru_a_param, "w_out": w_out, "norm_mlp_w": norm_mlp_w,
            "w_ff1": w_ff1, "w_ff2": w_ff2, "final_norm_w": final_norm_w}


def reference(x, norm_mix_w, w_in, gdn_conv_w, gdn_A_log, gdn_dt_bias, gdn_norm_w,
              lru_conv_w, lru_conv_b, lru_gate_a_w, lru_gate_a_b, lru_gate_x_w, lru_gate_x_b,
              lru_a_param, w_out, norm_mlp_w, w_ff1, w_ff2, final_norm_w):
    for layer in range(DEPTH):
        x = hybrid_layer(x, norm_mix_w[layer], w_in[layer], gdn_conv_w[layer], gdn_A_log[layer],
                         gdn_dt_bias[layer], gdn_norm_w[layer], lru_conv_w[layer], lru_conv_b[layer],
                         lru_gate_a_w[layer], lru_gate_a_b[layer], lru_gate_x_w[layer],
                         lru_gate_x_b[layer], lru_a_param[layer], w_out[layer], norm_mlp_w[layer],
                         w_ff1[layer], w_ff2[layer])
    return rmsnorm(x, final_norm_w)
```

```python
import functools
import math

import jax
import jax.numpy as jnp
from jax import lax
from jax.experimental import pallas as pl
from jax.experimental.pallas import tpu as pltpu

EPS = 1e-6
HEAD_DIM = 128
LANES = 128
GDN_CHUNK = 64
INV_BASE = 16
LRU_BLOCKS = 8
LRU_C = 8.0
CONV_WIDTH = 4
CONV_HALO = 8
VMEM_LIMIT_BYTES = 56 * 1024 * 1024

_BF16 = jnp.bfloat16
_F32 = jnp.float32
_NT_DIMS = (((1,), (1,)), ((), ()))


def _dot(a, b):
    return jnp.dot(a.astype(_BF16), b.astype(_BF16), preferred_element_type=_F32)


def _dot_nt(a, b):
    return lax.dot_general(a.astype(_BF16), b.astype(_BF16), _NT_DIMS,
                           preferred_element_type=_F32)


def _sigmoid(x):
    return 1.0 / (1.0 + jnp.exp(-x))


def _softplus(x):
    return jnp.maximum(x, 0.0) + jnp.log1p(jnp.exp(-jnp.abs(x)))


def _gelu_tanh(x):
    c = math.sqrt(2.0 / math.pi)
    return 0.5 * x * (1.0 + jnp.tanh(c * (x + 0.044715 * (x * x * x))))


def _const_spec(shape):
    nd = len(shape)
    return pl.BlockSpec(shape, lambda *_: (0,) * nd, pipeline_mode=pl.Buffered(1))


def _inproj_kernel(x_ref, nw_ref, wconv_ref, wgate_ref, wba_ref, cw_ref, lcb_ref,
                   wlru_ref, blru_ref, ap_ref, alog_ref, dtb_ref,
                   q_ref, k_ref, v_ref, z_ref, gb_ref, lru_ref,
                   pbuf, hcarry, *, gdn_width):
    ts = x_ref.shape[0]
    lru_width = lru_ref.shape[1]
    n_heads = gdn_width // HEAD_DIM
    first = pl.program_id(1) == 0

    @pl.when(first)
    def _():
        pbuf[0:CONV_HALO, :] = jnp.zeros((CONV_HALO, pbuf.shape[1]), _F32)
        hcarry[...] = jnp.zeros_like(hcarry)

    x = x_ref[...]
    h = x * lax.rsqrt(jnp.mean(x * x, axis=-1, keepdims=True) + EPS) * nw_ref[...]
    hb = h.astype(_BF16)

    pbuf[CONV_HALO:CONV_HALO + ts, :] = jnp.dot(hb, wconv_ref[...], preferred_element_type=_F32)
    y = None
    for j in range(CONV_WIDTH):
        off = CONV_HALO - (CONV_WIDTH - 1) + j
        term = cw_ref[j:j + 1, :] * pbuf[off:off + ts, :]
        y = term if y is None else y + term
    pbuf[0:CONV_HALO, :] = pbuf[ts:ts + CONV_HALO, :]

    qkv = y[:, :3 * gdn_width]
    qkv = qkv * _sigmoid(qkv)
    scale = HEAD_DIM ** -0.5
    for hd in range(n_heads):
        lo = hd * HEAD_DIM
        qh = qkv[:, lo:lo + HEAD_DIM]
        kh = qkv[:, gdn_width + lo:gdn_width + lo + HEAD_DIM]
        qn = lax.rsqrt(jnp.sum(qh * qh, axis=-1, keepdims=True) + EPS)
        kn = lax.rsqrt(jnp.sum(kh * kh, axis=-1, keepdims=True) + EPS)
        q_ref[:, lo:lo + HEAD_DIM] = (qh * qn * scale).astype(q_ref.dtype)
        k_ref[:, lo:lo + HEAD_DIM] = (kh * kn).astype(k_ref.dtype)
    v_ref[...] = qkv[:, 2 * gdn_width:3 * gdn_width].astype(v_ref.dtype)

    pg = jnp.dot(hb, wgate_ref[...], preferred_element_type=_F32)
    z_ref[...] = pg[:, :gdn_width].astype(z_ref.dtype)
    lru_gate = pg[:, gdn_width:]

    pba = jnp.dot(hb, wba_ref[...], preferred_element_type=_F32)
    lane = lax.broadcasted_iota(jnp.int32, pba.shape, 1)
    beta = _sigmoid(pba)
    g = -jnp.exp(alog_ref[...]) * _softplus(pba + dtb_ref[...])
    gb_ref[...] = jnp.where(lane < n_heads, beta, g)

    xr = y[:, 3 * gdn_width:] + lcb_ref[...]
    gates = _sigmoid(jnp.dot(xr.astype(_BF16), wlru_ref[...], preferred_element_type=_F32)
                     + blru_ref[...])
    r = gates[:, :lru_width]
    i = gates[:, lru_width:]
    log_a = -LRU_C * r * _softplus(-ap_ref[...])
    a = jnp.exp(log_a)
    b = jnp.sqrt(jnp.maximum(1.0 - jnp.exp(2.0 * log_a), 0.0)) * (i * xr)

    row = lax.broadcasted_iota(jnp.int32, a.shape, 0)
    d = 1
    while d < ts:
        keep = row >= d
        a_sh = jnp.where(keep, pltpu.roll(a, d, 0), 1.0)
        b_sh = jnp.where(keep, pltpu.roll(b, d, 0), 0.0)
        b = a * b_sh + b
        a = a * a_sh
        d *= 2
    hseq = b + a * hcarry[...]
    hcarry[...] = hseq[ts - 1:ts, :]
    lru_ref[...] = (hseq * _gelu_tanh(lru_gate)).astype(lru_ref.dtype)


def _inproj(x, nw, wconv, wgate, wba, cw, lcb, wlru, blru, ap, alog, dtb, *, ts, gdn_width):
    bsz, seq, d_model = x.shape
    lru_width = lcb.shape[1]
    conv_cols = wconv.shape[1]
    grid = (bsz, seq // ts)
    tile = lambda w: pl.BlockSpec((None, ts, w), lambda b, s: (b, s, 0))
    out_shape = (
        jax.ShapeDtypeStruct((bsz, seq, gdn_width), _BF16),
        jax.ShapeDtypeStruct((bsz, seq, gdn_width), _BF16),
        jax.ShapeDtypeStruct((bsz, seq, gdn_width), _BF16),
        jax.ShapeDtypeStruct((bsz, seq, gdn_width), _BF16),
        jax.ShapeDtypeStruct((bsz, seq, LANES), _F32),
        jax.ShapeDtypeStruct((bsz, seq, lru_width), _BF16),
    )
    consts = (nw, wconv, wgate, wba, cw, lcb, wlru, blru, ap, alog, dtb)
    return pl.pallas_call(
        functools.partial(_inproj_kernel, gdn_width=gdn_width),
        out_shape=out_shape,
        grid=grid,
        in_specs=[tile(d_model)] + [_const_spec(c.shape) for c in consts],
        out_specs=(tile(gdn_width), tile(gdn_width), tile(gdn_width), tile(gdn_width),
                   tile(LANES), tile(lru_width)),
        scratch_shapes=[pltpu.VMEM((ts + CONV_HALO, conv_cols), _F32),
                        pltpu.VMEM((1, lru_width), _F32)],
        compiler_params=pltpu.CompilerParams(
            dimension_semantics=("parallel", "arbitrary"),
            vmem_limit_bytes=VMEM_LIMIT_BYTES),
        name="inproj",
    )(x, *consts)


def _unit_lower_inverse(a, row, col):
    eye = (row == col).astype(_F32)
    d = jnp.where((row // INV_BASE) == (col // INV_BASE), a, 0.0)
    x = eye - d
    p = d
    span = 2
    while span < INV_BASE:
        p = _dot(p, p)
        x = x + _dot(x, p)
        span *= 2
    size = INV_BASE
    while size < GDN_CHUNK:
        e = jnp.where(((row // (2 * size)) == (col // (2 * size)))
                      & ((row // size) != (col // size)), a, 0.0)
        x = x - _dot(_dot(x, e), x)
        size *= 2
    return x


def _gdn_kernel(q_ref, k_ref, v_ref, z_ref, gb_ref, nw_ref, o_ref, state, *, n_heads):
    ts = q_ref.shape[0]
    n_chunks = ts // GDN_CHUNK

    @pl.when(pl.program_id(1) == 0)
    def _():
        state[...] = jnp.zeros_like(state)

    row = lax.broadcasted_iota(jnp.int32, (ts, ts), 0)
    col = lax.broadcasted_iota(jnp.int32, (ts, ts), 1)
    same_chunk = (row // GDN_CHUNK) == (col // GDN_CHUNK)
    causal = same_chunk & (row >= col)
    strict = same_chunk & (row > col)

    gb = gb_ref[...]
    gcum = jnp.dot(causal.astype(_F32), gb, preferred_element_type=_F32,
                   precision=lax.Precision.HIGHEST)
    gcum_t = gcum.T

    for hd in range(n_heads):
        lo = hd * HEAD_DIM
        beta = gb[:, hd:hd + 1]
        gcol = gcum[:, n_heads + hd:n_heads + hd + 1]
        grow = gcum_t[n_heads + hd:n_heads + hd + 1, :]
        decay = jnp.where(causal, jnp.exp(jnp.where(causal, gcol - grow, 0.0)), 0.0)

        q = q_ref[:, lo:lo + HEAD_DIM].astype(_F32)
        k = k_ref[:, lo:lo + HEAD_DIM].astype(_F32)
        v = v_ref[:, lo:lo + HEAD_DIM].astype(_F32)
        egc = jnp.exp(gcol)
        kb = k * beta
        a = jnp.where(strict, _dot_nt(kb, k) * decay, 0.0)
        t = _unit_lower_inverse(a, row, col)
        sol = _dot(t, jnp.concatenate([v * beta, kb * egc], axis=1))
        u = sol[:, :HEAD_DIM]
        w = sol[:, HEAD_DIM:]
        qk = _dot_nt(q, k) * decay
        qg = q * egc

        s = state[hd]
        outs = []
        for c in range(n_chunks):
            r0 = c * GDN_CHUNK
            r1 = r0 + GDN_CHUNK
            sb = s.astype(_BF16)
            v_new = u[r0:r1] - _dot(w[r0:r1], sb)
            outs.append(_dot(qg[r0:r1], sb) + _dot(qk[r0:r1, r0:r1], v_new))
            g_last = gcol[r1 - 1:r1, :]
            k_dec = k[r0:r1] * jnp.exp(g_last - gcol[r0:r1])
            s = s * jnp.exp(g_last) + _dot(k_dec.T, v_new)
        state[hd] = s

        o = jnp.concatenate(outs, axis=0)
        o = o * lax.rsqrt(jnp.mean(o * o, axis=-1, keepdims=True) + EPS) * nw_ref[...]
        z = z_ref[:, lo:lo + HEAD_DIM].astype(_F32)
        o_ref[:, lo:lo + HEAD_DIM] = (o * (z * _sigmoid(z))).astype(o_ref.dtype)


def _gdn(q, k, v, z, gb, nw, *, ts):
    bsz, seq, gdn_width = q.shape
    n_heads = gdn_width // HEAD_DIM
    tile = lambda w: pl.BlockSpec((None, ts, w), lambda b, s: (b, s, 0))
    return pl.pallas_call(
        functools.partial(_gdn_kernel, n_heads=n_heads),
        out_shape=jax.ShapeDtypeStruct((bsz, seq, gdn_width), _BF16),
        grid=(bsz, seq // ts),
        in_specs=[tile(gdn_width)] * 4 + [tile(LANES), _const_spec(nw.shape)],
        out_specs=tile(gdn_width),
        scratch_shapes=[pltpu.VMEM((n_heads, HEAD_DIM, HEAD_DIM), _F32)],
        compiler_params=pltpu.CompilerParams(
            dimension_semantics=("parallel", "arbitrary"),
            vmem_limit_bytes=VMEM_LIMIT_BYTES),
        name="gdn",
    )(q, k, v, z, gb, nw)


def _mlp_kernel(x_ref, gdn_ref, lru_ref, wog_ref, wol_ref, nw_ref, w1_ref, w2_ref, fnw_ref,
                o_ref, *, ff_chunk):
    d_ff = w1_ref.shape[1]
    x1 = (x_ref[...]
          + jnp.dot(gdn_ref[...], wog_ref[...], preferred_element_type=_F32)
          + jnp.dot(lru_ref[...], wol_ref[...], preferred_element_type=_F32))
    m = x1 * lax.rsqrt(jnp.mean(x1 * x1, axis=-1, keepdims=True) + EPS) * nw_ref[...]
    mb = m.astype(_BF16)
    ff = None
    for c in range(d_ff // ff_chunk):
        c0 = c * ff_chunk
        u = jnp.maximum(jnp.dot(mb, w1_ref[:, c0:c0 + ff_chunk], preferred_element_type=_F32), 0.0)
        t = jnp.dot((u * u).astype(_BF16), w2_ref[c0:c0 + ff_chunk, :],
                    preferred_element_type=_F32)
        ff = t if ff is None else ff + t
    x2 = x1 + ff
    o_ref[...] = x2 * lax.rsqrt(jnp.mean(x2 * x2, axis=-1, keepdims=True) + EPS) * fnw_ref[...]


def _mlp(x, gdn, lru, wog, wol, nw, w1, w2, fnw, *, tm, ff_chunk):
    n_tok, d_model = x.shape
    tile = lambda w: pl.BlockSpec((tm, w), lambda i: (i, 0))
    consts = (wog, wol, nw, w1, w2, fnw)
    return pl.pallas_call(
        functools.partial(_mlp_kernel, ff_chunk=ff_chunk),
        out_shape=jax.ShapeDtypeStruct((n_tok, d_model), _F32),
        grid=(n_tok // tm,),
        in_specs=[tile(d_model), tile(gdn.shape[1]), tile(lru.shape[1])]
                 + [_const_spec(c.shape) for c in consts],
        out_specs=tile(d_model),
        compiler_params=pltpu.CompilerParams(
            dimension_semantics=("parallel",),
            vmem_limit_bytes=VMEM_LIMIT_BYTES),
        name="outproj_mlp",
    )(x, gdn, lru, *consts)


def _block_diag(w):
    g, i, j = w.shape
    eye = jnp.eye(g, dtype=w.dtype)
    return (eye[:, None, :, None] * w[:, :, None, :]).reshape(g * i, g * j)


def _layer(x, norm_mix_w, w_in, gdn_conv_w, gdn_A_log, gdn_dt_bias, gdn_norm_w,
           lru_conv_w, lru_conv_b, lru_gate_a_w, lru_gate_a_b, lru_gate_x_w, lru_gate_x_b,
           lru_a_param, w_out, norm_mlp_w, w_ff1, w_ff2, final_norm_w, *, seq_tile, tok_tile,
           ff_chunk):
    bsz, seq, d_model = x.shape
    n_heads = gdn_A_log.shape[0]
    gdn_width = n_heads * HEAD_DIM
    lru_width = lru_conv_b.shape[0]
    assert 2 * n_heads <= LANES
    assert seq % seq_tile == 0 and seq_tile % GDN_CHUNK == 0
    assert (bsz * seq) % tok_tile == 0 and w_ff1.shape[1] % ff_chunk == 0

    o_qkv, o_z = 0, 3 * gdn_width
    o_b = o_z + gdn_width
    o_a = o_b + n_heads
    o_lx = o_a + n_heads
    o_lg = o_lx + lru_width
    wconv = jnp.concatenate([w_in[:, o_qkv:o_z], w_in[:, o_lx:o_lg]], axis=1).astype(_BF16)
    wgate = jnp.concatenate([w_in[:, o_z:o_b], w_in[:, o_lg:o_lg + lru_width]], axis=1).astype(_BF16)
    wba = jnp.pad(w_in[:, o_b:o_lx], ((0, 0), (0, LANES - 2 * n_heads))).astype(_BF16)
    cw = jnp.concatenate([gdn_conv_w, lru_conv_w], axis=1).astype(_F32)
    pad_heads = lambda p: jnp.pad(p.astype(_F32), (n_heads, LANES - 2 * n_heads)).reshape(1, LANES)
    wlru = jnp.concatenate([_block_diag(lru_gate_a_w), _block_diag(lru_gate_x_w)], axis=1).astype(_BF16)
    blru = jnp.concatenate([lru_gate_a_b.reshape(1, -1), lru_gate_x_b.reshape(1, -1)], axis=1)

    q, k, v, z, gb, lru = _inproj(
        x, norm_mix_w.reshape(1, -1), wconv, wgate, wba, cw, lru_conv_b.reshape(1, -1),
        wlru, blru, lru_a_param.reshape(1, -1), pad_heads(gdn_A_log), pad_heads(gdn_dt_bias),
        ts=seq_tile, gdn_width=gdn_width)
    gdn = _gdn(q, k, v, z, gb, gdn_norm_w.reshape(1, -1), ts=seq_tile)

    n_tok = bsz * seq
    out = _mlp(x.reshape(n_tok, d_model), gdn.reshape(n_tok, gdn_width),
               lru.reshape(n_tok, lru_width),
               w_out[:gdn_width].astype(_BF16), w_out[gdn_width:].astype(_BF16),
               norm_mlp_w.reshape(1, -1), w_ff1.astype(_BF16), w_ff2.astype(_BF16),
               final_norm_w.reshape(1, -1), tm=tok_tile, ff_chunk=ff_chunk)
    return out.reshape(bsz, seq, d_model)


def kernel(x, norm_mix_w, w_in, gdn_conv_w, gdn_A_log, gdn_dt_bias, gdn_norm_w, lru_conv_w, lru_conv_b, lru_gate_a_w, lru_gate_a_b, lru_gate_x_w, lru_gate_x_b, lru_a_param, w_out, norm_mlp_w, w_ff1, w_ff2, final_norm_w):
    assert norm_mix_w.shape[0] == 1, "single-layer stack"
    return _layer(x, norm_mix_w[0], w_in[0], gdn_conv_w[0], gdn_A_log[0], gdn_dt_bias[0],
                  gdn_norm_w[0], lru_conv_w[0], lru_conv_b[0], lru_gate_a_w[0], lru_gate_a_b[0],
                  lru_gate_x_w[0], lru_gate_x_b[0], lru_a_param[0], w_out[0], norm_mlp_w[0],
                  w_ff1[0], w_ff2[0], final_norm_w, seq_tile=256, tok_tile=512, ff_chunk=1024)
```

```python
import functools
import math

import jax
import jax.numpy as jnp
from jax import lax
from jax.experimental import pallas as pl
from jax.experimental.pallas import tpu as pltpu

EPS = 1e-6
HEAD_DIM = 128
LANES = 128
GDN_CHUNK = 64
INV_BASE = 16
LRU_BLOCKS = 8
LRU_C = 8.0
CONV_WIDTH = 4
CONV_HALO = 8
VMEM_LIMIT_BYTES = 56 * 1024 * 1024

_BF16 = jnp.bfloat16
_F32 = jnp.float32
_NT_DIMS = (((1,), (1,)), ((), ()))


def _dot(a, b):
    return jnp.dot(a.astype(_BF16), b.astype(_BF16), preferred_element_type=_F32)


def _dot_nt(a, b):
    return lax.dot_general(a.astype(_BF16), b.astype(_BF16), _NT_DIMS,
                           preferred_element_type=_F32)


def _sigmoid(x):
    return 1.0 / (1.0 + jnp.exp(-x))


def _softplus(x):
    return jnp.maximum(x, 0.0) + jnp.log1p(jnp.exp(-jnp.abs(x)))


def _gelu_tanh(x):
    c = math.sqrt(2.0 / math.pi)
    return 0.5 * x * (1.0 + jnp.tanh(c * (x + 0.044715 * (x * x * x))))


def _const_spec(shape):
    nd = len(shape)
    return pl.BlockSpec(shape, lambda *_: (0,) * nd, pipeline_mode=pl.Buffered(1))


def _inproj_kernel(x_ref, nw_ref, wconv_ref, wgate_ref, wba_ref, cw_ref, lcb_ref,
                   wlru_ref, blru_ref, ap_ref, alog_ref, dtb_ref,
                   q_ref, k_ref, v_ref, z_ref, gb_ref, lru_ref,
                   pbuf, hcarry, *, gdn_width):
    ts = x_ref.shape[0]
    lru_width = lru_ref.shape[1]
    n_heads = gdn_width // HEAD_DIM
    first = pl.program_id(1) == 0

    @pl.when(first)
    def _():
        pbuf[0:CONV_HALO, :] = jnp.zeros((CONV_HALO, pbuf.shape[1]), _F32)
        hcarry[...] = jnp.zeros_like(hcarry)

    x = x_ref[...]
    h = x * lax.rsqrt(jnp.mean(x * x, axis=-1, keepdims=True) + EPS) * nw_ref[...]
    hb = h.astype(_BF16)

    pbuf[CONV_HALO:CONV_HALO + ts, :] = jnp.dot(hb, wconv_ref[...], preferred_element_type=_F32)
    y = None
    for j in range(CONV_WIDTH):
        off = CONV_HALO - (CONV_WIDTH - 1) + j
        term = cw_ref[j:j + 1, :] * pbuf[off:off + ts, :]
        y = term if y is None else y + term
    pbuf[0:CONV_HALO, :] = pbuf[ts:ts + CONV_HALO, :]

    qkv = y[:, :3 * gdn_width]
    qkv = qkv * _sigmoid(qkv)
    scale = HEAD_DIM ** -0.5
    for hd in range(n_heads):
        lo = hd * HEAD_DIM
        qh = qkv[:, lo:lo + HEAD_DIM]
        kh = qkv[:, gdn_width + lo:gdn_width + lo + HEAD_DIM]
        qn = lax.rsqrt(jnp.sum(qh * qh, axis=-1, keepdims=True) + EPS)
        kn = lax.rsqrt(jnp.sum(kh * kh, axis=-1, keepdims=True) + EPS)
        q_ref[:, lo:lo + HEAD_DIM] = (qh * qn * scale).astype(q_ref.dtype)
        k_ref[:, lo:lo + HEAD_DIM] = (kh * kn).astype(k_ref.dtype)
    v_ref[...] = qkv[:, 2 * gdn_width:3 * gdn_width].astype(v_ref.dtype)

    pg = jnp.dot(hb, wgate_ref[...], preferred_element_type=_F32)
    z_ref[...] = pg[:, :gdn_width].astype(z_ref.dtype)
    lru_gate = pg[:, gdn_width:]

    pba = jnp.dot(hb, wba_ref[...], preferred_element_type=_F32)
    lane = lax.broadcasted_iota(jnp.int32, pba.shape, 1)
    beta = _sigmoid(pba)
    g = -jnp.exp(alog_ref[...]) * _softplus(pba + dtb_ref[...])
    gb_ref[...] = jnp.where(lane < n_heads, beta, g)

    xr = y[:, 3 * gdn_width:] + lcb_ref[...]
    gates = _sigmoid(jnp.dot(xr.astype(_BF16), wlru_ref[...], preferred_element_type=_F32)
                     + blru_ref[...])
    r = gates[:, :lru_width]
    i = gates[:, lru_width:]
    log_a = -LRU_C * r * _softplus(-ap_ref[...])
    a = jnp.exp(log_a)
    b = jnp.sqrt(jnp.maximum(1.0 - jnp.exp(2.0 * log_a), 0.0)) * (i * xr)

    row = lax.broadcasted_iota(jnp.int32, a.shape, 0)
    d = 1
    while d < ts:
        keep = row >= d
        a_sh = jnp.where(keep, pltpu.roll(a, d, 0), 1.0)
        b_sh = jnp.where(keep, pltpu.roll(b, d, 0), 0.0)
        b = a * b_sh + b
        a = a * a_sh
        d *= 2
    hseq = b + a * hcarry[...]
    hcarry[...] = hseq[ts - 1:ts, :]
    lru_ref[...] = (hseq * _gelu_tanh(lru_gate)).astype(lru_ref.dtype)


def _inproj(x, nw, wconv, wgate, wba, cw, lcb, wlru, blru, ap, alog, dtb, *, ts, gdn_width):
    bsz, seq, d_model = x.shape
    lru_width = lcb.shape[1]
    conv_cols = wconv.shape[1]
    grid = (bsz, seq // ts)
    tile = lambda w: pl.BlockSpec((None, ts, w), lambda b, s: (b, s, 0))
    out_shape = (
        jax.ShapeDtypeStruct((bsz, seq, gdn_width), _BF16),
        jax.ShapeDtypeStruct((bsz, seq, gdn_width), _BF16),
        jax.ShapeDtypeStruct((bsz, seq, gdn_width), _BF16),
        jax.ShapeDtypeStruct((bsz, seq, gdn_width), _BF16),
        jax.ShapeDtypeStruct((bsz, seq, LANES), _F32),
        jax.ShapeDtypeStruct((bsz, seq, lru_width), _BF16),
    )
    consts = (nw, wconv, wgate, wba, cw, lcb, wlru, blru, ap, alog, dtb)
    return pl.pallas_call(
        functools.partial(_inproj_kernel, gdn_width=gdn_width),
        out_shape=out_shape,
        grid=grid,
        in_specs=[tile(d_model)] + [_const_spec(c.shape) for c in consts],
        out_specs=(tile(gdn_width), tile(gdn_width), tile(gdn_width), tile(gdn_width),
                   tile(LANES), tile(lru_width)),
        scratch_shapes=[pltpu.VMEM((ts + CONV_HALO, conv_cols), _F32),
                        pltpu.VMEM((1, lru_width), _F32)],
        compiler_params=pltpu.CompilerParams(
            dimension_semantics=("parallel", "arbitrary"),
            vmem_limit_bytes=VMEM_LIMIT_BYTES),
        name="inproj",
    )(x, *consts)


def _unit_lower_inverse(a_heads, row, col):
    eye = (row == col).astype(_F32)
    base = (row // INV_BASE) == (col // INV_BASE)
    p = [jnp.where(base, a, 0.0) for a in a_heads]
    x = [eye - d for d in p]
    span = 2
    while span < INV_BASE:
        p = [_dot(ph, ph) for ph in p]
        x = [xh + _dot(xh, ph) for xh, ph in zip(x, p)]
        span *= 2
    size = INV_BASE
    while size < GDN_CHUNK:
        off = ((row // (2 * size)) == (col // (2 * size))) & ((row // size) != (col // size))
        xe = [_dot(xh, jnp.where(off, a, 0.0)) for xh, a in zip(x, a_heads)]
        x = [xh - _dot(xeh, xh) for xh, xeh in zip(x, xe)]
        size *= 2
    return x


def _gdn_kernel(q_ref, k_ref, v_ref, z_ref, gb_ref, nw_ref, o_ref, state, *, n_heads):
    ts = q_ref.shape[0]
    n_chunks = ts // GDN_CHUNK
    heads = range(n_heads)

    @pl.when(pl.program_id(1) == 0)
    def _():
        state[...] = jnp.zeros_like(state)

    row = lax.broadcasted_iota(jnp.int32, (ts, ts), 0)
    col = lax.broadcasted_iota(jnp.int32, (ts, ts), 1)
    same_chunk = (row // GDN_CHUNK) == (col // GDN_CHUNK)
    causal = same_chunk & (row >= col)
    strict = same_chunk & (row > col)

    gb = gb_ref[...]
    gcum = jnp.dot(causal.astype(_F32), gb, preferred_element_type=_F32,
                   precision=lax.Precision.HIGHEST)
    gcum_t = gcum.T

    k_all, gcol_all, a_all, rhs_all, qk_all, qg_all = [], [], [], [], [], []
    for hd in heads:
        lo = hd * HEAD_DIM
        beta = gb[:, hd:hd + 1]
        gcol = gcum[:, n_heads + hd:n_heads + hd + 1]
        grow = gcum_t[n_heads + hd:n_heads + hd + 1, :]
        decay = jnp.where(causal, jnp.exp(jnp.where(causal, gcol - grow, 0.0)), 0.0)
        q = q_ref[:, lo:lo + HEAD_DIM].astype(_F32)
        k = k_ref[:, lo:lo + HEAD_DIM].astype(_F32)
        v = v_ref[:, lo:lo + HEAD_DIM].astype(_F32)
        egc = jnp.exp(gcol)
        kb = k * beta
        k_all.append(k)
        gcol_all.append(gcol)
        a_all.append(jnp.where(strict, _dot_nt(kb, k) * decay, 0.0))
        rhs_all.append(jnp.concatenate([v * beta, kb * egc], axis=1))
        qk_all.append(_dot_nt(q, k) * decay)
        qg_all.append(q * egc)
    t_all = _unit_lower_inverse(a_all, row, col)
    sol_all = [_dot(t, rhs) for t, rhs in zip(t_all, rhs_all)]

    lhs_all = [[] for _ in heads]
    n_all = [[] for _ in heads]
    o2_all = [[] for _ in heads]
    gam_all = [[] for _ in heads]
    for c in range(n_chunks):
        r0 = c * GDN_CHUNK
        r1 = r0 + GDN_CHUNK
        for hd in heads:
            sol = sol_all[hd][r0:r1]
            g_last = gcol_all[hd][r1 - 1:r1, :]
            k_dec = k_all[hd][r0:r1] * jnp.exp(g_last - gcol_all[hd][r0:r1])
            kd_sol = _dot(k_dec.T, sol)
            qk_sol = _dot(qk_all[hd][r0:r1, r0:r1], sol)
            lhs_all[hd].append(jnp.concatenate(
                [kd_sol[:, HEAD_DIM:], qg_all[hd][r0:r1] - qk_sol[:, HEAD_DIM:]], axis=0))
            n_all[hd].append(kd_sol[:, :HEAD_DIM])
            o2_all[hd].append(qk_sol[:, :HEAD_DIM])
            gam_all[hd].append(jnp.exp(g_last))

    s_all = [state[hd] for hd in heads]
    outs = [[] for _ in heads]
    for c in range(n_chunks):
        for hd in heads:
            s = s_all[hd]
            r = _dot(lhs_all[hd][c], s)
            outs[hd].append(r[HEAD_DIM:] + o2_all[hd][c])
            s_all[hd] = s * gam_all[hd][c] - r[:HEAD_DIM] + n_all[hd][c]

    for hd in heads:
        lo = hd * HEAD_DIM
        state[hd] = s_all[hd]
        o = jnp.concatenate(outs[hd], axis=0)
        o = o * lax.rsqrt(jnp.mean(o * o, axis=-1, keepdims=True) + EPS) * nw_ref[...]
        z = z_ref[:, lo:lo + HEAD_DIM].astype(_F32)
        o_ref[:, lo:lo + HEAD_DIM] = (o * (z * _sigmoid(z))).astype(o_ref.dtype)


def _gdn(q, k, v, z, gb, nw, *, ts):
    bsz, seq, gdn_width = q.shape
    n_heads = gdn_width // HEAD_DIM
    tile = lambda w: pl.BlockSpec((None, ts, w), lambda b, s: (b, s, 0))
    return pl.pallas_call(
        functools.partial(_gdn_kernel, n_heads=n_heads),
        out_shape=jax.ShapeDtypeStruct((bsz, seq, gdn_width), _BF16),
        grid=(bsz, seq // ts),
        in_specs=[tile(gdn_width)] * 4 + [tile(LANES), _const_spec(nw.shape)],
        out_specs=tile(gdn_width),
        scratch_shapes=[pltpu.VMEM((n_heads, HEAD_DIM, HEAD_DIM), _F32)],
        compiler_params=pltpu.CompilerParams(
            dimension_semantics=("parallel", "arbitrary"),
            vmem_limit_bytes=VMEM_LIMIT_BYTES),
        name="gdn",
    )(q, k, v, z, gb, nw)


def _mlp_kernel(x_ref, gdn_ref, lru_ref, wog_ref, wol_ref, nw_ref, w1_ref, w2_ref, fnw_ref,
                o_ref, *, ff_chunk):
    d_ff = w1_ref.shape[1]
    x1 = (x_ref[...]
          + jnp.dot(gdn_ref[...], wog_ref[...], preferred_element_type=_F32)
          + jnp.dot(lru_ref[...], wol_ref[...], preferred_element_type=_F32))
    m = x1 * lax.rsqrt(jnp.mean(x1 * x1, axis=-1, keepdims=True) + EPS) * nw_ref[...]
    mb = m.astype(_BF16)
    ff = None
    for c in range(d_ff // ff_chunk):
        c0 = c * ff_chunk
        u = jnp.maximum(jnp.dot(mb, w1_ref[:, c0:c0 + ff_chunk], preferred_element_type=_F32), 0.0)
        t = jnp.dot((u * u).astype(_BF16), w2_ref[c0:c0 + ff_chunk, :],
                    preferred_element_type=_F32)
        ff = t if ff is None else ff + t
    x2 = x1 + ff
    o_ref[...] = x2 * lax.rsqrt(jnp.mean(x2 * x2, axis=-1, keepdims=True) + EPS) * fnw_ref[...]


def _mlp(x, gdn, lru, wog, wol, nw, w1, w2, fnw, *, tm, ff_chunk):
    n_tok, d_model = x.shape
    tile = lambda w: pl.BlockSpec((tm, w), lambda i: (i, 0))
    consts = (wog, wol, nw, w1, w2, fnw)
    return pl.pallas_call(
        functools.partial(_mlp_kernel, ff_chunk=ff_chunk),
        out_shape=jax.ShapeDtypeStruct((n_tok, d_model), _F32),
        grid=(n_tok // tm,),
        in_specs=[tile(d_model), tile(gdn.shape[1]), tile(lru.shape[1])]
                 + [_const_spec(c.shape) for c in consts],
        out_specs=tile(d_model),
        compiler_params=pltpu.CompilerParams(
            dimension_semantics=("parallel",),
            vmem_limit_bytes=VMEM_LIMIT_BYTES),
        name="outproj_mlp",
    )(x, gdn, lru, *consts)


def _block_diag(w):
    g, i, j = w.shape
    eye = jnp.eye(g, dtype=w.dtype)
    return (eye[:, None, :, None] * w[:, :, None, :]).reshape(g * i, g * j)


def _layer(x, norm_mix_w, w_in, gdn_conv_w, gdn_A_log, gdn_dt_bias, gdn_norm_w,
           lru_conv_w, lru_conv_b, lru_gate_a_w, lru_gate_a_b, lru_gate_x_w, lru_gate_x_b,
           lru_a_param, w_out, norm_mlp_w, w_ff1, w_ff2, final_norm_w, *, seq_tile, tok_tile,
           ff_chunk):
    bsz, seq, d_model = x.shape
    n_heads = gdn_A_log.shape[0]
    gdn_width = n_heads * HEAD_DIM
    lru_width = lru_conv_b.shape[0]
    assert 2 * n_heads <= LANES
    assert seq % seq_tile == 0 and seq_tile % GDN_CHUNK == 0
    assert (bsz * seq) % tok_tile == 0 and w_ff1.shape[1] % ff_chunk == 0

    o_qkv, o_z = 0, 3 * gdn_width
    o_b = o_z + gdn_width
    o_a = o_b + n_heads
    o_lx = o_a + n_heads
    o_lg = o_lx + lru_width
    wconv = jnp.concatenate([w_in[:, o_qkv:o_z], w_in[:, o_lx:o_lg]], axis=1).astype(_BF16)
    wgate = jnp.concatenate([w_in[:, o_z:o_b], w_in[:, o_lg:o_lg + lru_width]], axis=1).astype(_BF16)
    wba = jnp.pad(w_in[:, o_b:o_lx], ((0, 0), (0, LANES - 2 * n_heads))).astype(_BF16)
    cw = jnp.concatenate([gdn_conv_w, lru_conv_w], axis=1).astype(_F32)
    pad_heads = lambda p: jnp.pad(p.astype(_F32), (n_heads, LANES - 2 * n_heads)).reshape(1, LANES)
    wlru = jnp.concatenate([_block_diag(lru_gate_a_w), _block_diag(lru_gate_x_w)], axis=1).astype(_BF16)
    blru = jnp.concatenate([lru_gate_a_b.reshape(1, -1), lru_gate_x_b.reshape(1, -1)], axis=1)

    q, k, v, z, gb, lru = _inproj(
        x, norm_mix_w.reshape(1, -1), wconv, wgate, wba, cw, lru_conv_b.reshape(1, -1),
        wlru, blru, lru_a_param.reshape(1, -1), pad_heads(gdn_A_log), pad_heads(gdn_dt_bias),
        ts=seq_tile, gdn_width=gdn_width)
    gdn = _gdn(q, k, v, z, gb, gdn_norm_w.reshape(1, -1), ts=seq_tile)

    n_tok = bsz * seq
    out = _mlp(x.reshape(n_tok, d_model), gdn.reshape(n_tok, gdn_width),
               lru.reshape(n_tok, lru_width),
               w_out[:gdn_width].astype(_BF16), w_out[gdn_width:].astype(_BF16),
               norm_mlp_w.reshape(1, -1), w_ff1.astype(_BF16), w_ff2.astype(_BF16),
               final_norm_w.reshape(1, -1), tm=tok_tile, ff_chunk=ff_chunk)
    return out.reshape(bsz, seq, d_model)


def kernel(x, norm_mix_w, w_in, gdn_conv_w, gdn_A_log, gdn_dt_bias, gdn_norm_w, lru_conv_w, lru_conv_b, lru_gate_a_w, lru_gate_a_b, lru_gate_x_w, lru_gate_x_b, lru_a_param, w_out, norm_mlp_w, w_ff1, w_ff2, final_norm_w):
    assert norm_mix_w.shape[0] == 1, "single-layer stack"
    return _layer(x, norm_mix_w[0], w_in[0], gdn_conv_w[0], gdn_A_log[0], gdn_dt_bias[0],
                  gdn_norm_w[0], lru_conv_w[0], lru_conv_b[0], lru_gate_a_w[0], lru_gate_a_b[0],
                  lru_gate_x_w[0], lru_gate_x_b[0], lru_a_param[0], w_out[0], norm_mlp_w[0],
                  w_ff1[0], w_ff2[0], final_norm_w, seq_tile=256, tok_tile=512, ff_chunk=1024)
```

```python
import functools
import math

import jax
import jax.numpy as jnp
from jax import lax
from jax.experimental import pallas as pl
from jax.experimental.pallas import tpu as pltpu

EPS = 1e-6
HEAD_DIM = 128
LANES = 128
SUBLANES = 8
GDN_CHUNK = 64
INV_BASE = 16
LRU_C = 8.0
CONV_WIDTH = 4
CONV_HALO = 8
VMEM_LIMIT_BYTES = 56 * 1024 * 1024

_BF16 = jnp.bfloat16
_F32 = jnp.float32
_NT_DIMS = (((1,), (1,)), ((), ()))


def _dot(a, b):
    return jnp.dot(a.astype(_BF16), b.astype(_BF16), preferred_element_type=_F32)


def _dot_nt(a, b):
    return lax.dot_general(a.astype(_BF16), b.astype(_BF16), _NT_DIMS,
                           preferred_element_type=_F32)


def _sigmoid(x):
    return 1.0 / (1.0 + jnp.exp(-x))


def _softplus(x):
    return jnp.maximum(x, 0.0) + jnp.log1p(jnp.exp(-jnp.abs(x)))


def _gelu_tanh(x):
    c = math.sqrt(2.0 / math.pi)
    return 0.5 * x * (1.0 + jnp.tanh(c * (x + 0.044715 * (x * x * x))))


def _rmsnorm(x, w):
    return x * lax.rsqrt(jnp.mean(x * x, axis=-1, keepdims=True) + EPS) * w


def _const_spec(shape):
    nd = len(shape)
    return pl.BlockSpec(shape, lambda *_: (0,) * nd, pipeline_mode=pl.Buffered(1))


def _causal_conv(pbuf, cw_ref, ts):
    pe = pbuf[...]
    pd = pltpu.roll(pe, 1, 0)
    near = cw_ref[3:4, :] * pe + cw_ref[2:3, :] * pd
    far = cw_ref[1:2, :] * pe + cw_ref[0:1, :] * pd
    y = near + pltpu.roll(far, 2, 0)
    return y[CONV_HALO:CONV_HALO + ts, :]


def _lru_branch(xr, lru_gate, wlru_ref, blru_ref, ap_ref, hcarry):
    ts, lru_width = xr.shape
    gates = _sigmoid(jnp.dot(xr.astype(_BF16), wlru_ref[...], preferred_element_type=_F32)
                     + blru_ref[...])
    r = gates[:, :lru_width]
    i = gates[:, lru_width:]
    log_a = -LRU_C * r * _softplus(-ap_ref[...])
    a = jnp.exp(log_a)
    b = jnp.sqrt(jnp.maximum(1.0 - jnp.exp(2.0 * log_a), 0.0)) * (i * xr)

    row = lax.broadcasted_iota(jnp.int32, a.shape, 0) % SUBLANES
    d = 1
    while d < SUBLANES:
        keep = row >= d
        a_sh = jnp.where(keep, pltpu.roll(a, d, 0), 1.0)
        b_sh = jnp.where(keep, pltpu.roll(b, d, 0), 0.0)
        b = a * b_sh + b
        a = a * a_sh
        d *= 2
    carry = hcarry[...]
    groups = []
    for r0 in range(0, ts, SUBLANES):
        hg = b[r0:r0 + SUBLANES] + a[r0:r0 + SUBLANES] * carry
        carry = hg[SUBLANES - 1:SUBLANES]
        groups.append(hg)
    hcarry[...] = carry
    return jnp.concatenate(groups, axis=0) * _gelu_tanh(lru_gate)


def _unit_lower_inverse(a_heads, row, col):
    eye = (row == col).astype(_F32)
    base = (row // INV_BASE) == (col // INV_BASE)
    p = [jnp.where(base, a, 0.0) for a in a_heads]
    x = [eye - d for d in p]
    span = 2
    while span < INV_BASE:
        p = [_dot(ph, ph) for ph in p]
        x = [xh + _dot(xh, ph) for xh, ph in zip(x, p)]
        span *= 2
    size = INV_BASE
    while size < GDN_CHUNK:
        off = ((row // (2 * size)) == (col // (2 * size))) & ((row // size) != (col // size))
        xe = [_dot(xh, jnp.where(off, a, 0.0)) for xh, a in zip(x, a_heads)]
        x = [xh - _dot(xeh, xh) for xh, xeh in zip(x, xe)]
        size *= 2
    return x


def _gdn_chunk_operands(q_all, k_all, v_all, gb):
    n_heads = len(q_all)
    heads = range(n_heads)
    ts = gb.shape[0]
    n_chunks = ts // GDN_CHUNK
    row = lax.broadcasted_iota(jnp.int32, (ts, ts), 0)
    col = lax.broadcasted_iota(jnp.int32, (ts, ts), 1)
    same_chunk = (row // GDN_CHUNK) == (col // GDN_CHUNK)
    causal = same_chunk & (row >= col)
    strict = same_chunk & (row > col)

    gcum = jnp.dot(causal.astype(_F32), gb, preferred_element_type=_F32,
                   precision=lax.Precision.HIGHEST)
    gcum_t = gcum.T

    gcol_all, a_all, rhs_all, qk_all, qg_all = [], [], [], [], []
    for hd in heads:
        beta = gb[:, hd:hd + 1]
        gcol = gcum[:, n_heads + hd:n_heads + hd + 1]
        grow = gcum_t[n_heads + hd:n_heads + hd + 1, :]
        decay = jnp.where(causal, jnp.exp(jnp.where(causal, gcol - grow, 0.0)), 0.0)
        q, k, v = q_all[hd], k_all[hd], v_all[hd]
        egc = jnp.exp(gcol)
        kb = k * beta
        gcol_all.append(gcol)
        a_all.append(jnp.where(strict, _dot_nt(kb, k) * decay, 0.0))
        rhs_all.append(jnp.concatenate([v * beta, kb * egc], axis=1))
        qk_all.append(_dot_nt(q, k) * decay)
        qg_all.append(q * egc)
    t_all = _unit_lower_inverse(a_all, row, col)
    sol_all = [_dot(t, rhs) for t, rhs in zip(t_all, rhs_all)]

    lhs_all = [[] for _ in heads]
    n_all = [[] for _ in heads]
    o2_all = [[] for _ in heads]
    gam_all = [[] for _ in heads]
    for c in range(n_chunks):
        r0 = c * GDN_CHUNK
        r1 = r0 + GDN_CHUNK
        for hd in heads:
            sol = sol_all[hd][r0:r1]
            g_last = gcol_all[hd][r1 - 1:r1, :]
            k_dec = k_all[hd][r0:r1] * jnp.exp(g_last - gcol_all[hd][r0:r1])
            kd_sol = _dot(k_dec.T, sol)
            qk_sol = _dot(qk_all[hd][r0:r1, r0:r1], sol)
            lhs_all[hd].append(jnp.concatenate(
                [kd_sol[:, HEAD_DIM:], qg_all[hd][r0:r1] - qk_sol[:, HEAD_DIM:]], axis=0))
            n_all[hd].append(kd_sol[:, :HEAD_DIM])
            o2_all[hd].append(qk_sol[:, :HEAD_DIM])
            gam_all[hd].append(jnp.exp(g_last))
    return lhs_all, n_all, o2_all, gam_all


def _gdn_recurrence(operands, state):
    lhs_all, n_all, o2_all, gam_all = operands
    heads = range(len(lhs_all))
    s_all = [state[hd] for hd in heads]
    outs = [[] for _ in heads]
    for c in range(len(lhs_all[0])):
        for hd in heads:
            s = s_all[hd]
            r = _dot(lhs_all[hd][c], s)
            outs[hd].append(r[HEAD_DIM:] + o2_all[hd][c])
            s_all[hd] = s * gam_all[hd][c] - r[:HEAD_DIM] + n_all[hd][c]
    for hd in heads:
        state[hd] = s_all[hd]
    return [jnp.concatenate(o, axis=0) for o in outs]


def _mixer_kernel(x_ref, nw_ref, wconv_ref, wgate_ref, wba_ref, cw_ref, lcb_ref,
                  wlru_ref, blru_ref, ap_ref, alog_ref, dtb_ref, gnw_ref,
                  gdn_ref, lru_ref, pbuf, hcarry, state):
    ts = x_ref.shape[0]
    gdn_width = gdn_ref.shape[1]
    n_heads = gdn_width // HEAD_DIM

    @pl.when(pl.program_id(1) == 0)
    def _():
        pbuf[0:CONV_HALO, :] = jnp.zeros((CONV_HALO, pbuf.shape[1]), _F32)
        hcarry[...] = jnp.zeros_like(hcarry)
        state[...] = jnp.zeros_like(state)

    hb = _rmsnorm(x_ref[...], nw_ref[...]).astype(_BF16)

    pbuf[CONV_HALO:CONV_HALO + ts, :] = jnp.dot(hb, wconv_ref[...], preferred_element_type=_F32)
    y = _causal_conv(pbuf, cw_ref, ts)
    pbuf[0:CONV_HALO, :] = pbuf[ts:ts + CONV_HALO, :]

    pg = jnp.dot(hb, wgate_ref[...], preferred_element_type=_F32)

    pba = jnp.dot(hb, wba_ref[...], preferred_element_type=_F32)
    lane = lax.broadcasted_iota(jnp.int32, pba.shape, 1)
    gb = jnp.where(lane < n_heads, _sigmoid(pba),
                   -jnp.exp(alog_ref[...]) * _softplus(pba + dtb_ref[...]))

    qkv = y[:, :3 * gdn_width]
    qkv = qkv * _sigmoid(qkv)
    scale = HEAD_DIM ** -0.5
    q_all, k_all, v_all = [], [], []
    for hd in range(n_heads):
        lo = hd * HEAD_DIM
        qh = qkv[:, lo:lo + HEAD_DIM]
        kh = qkv[:, gdn_width + lo:gdn_width + lo + HEAD_DIM]
        q_all.append(qh * (lax.rsqrt(jnp.sum(qh * qh, axis=-1, keepdims=True) + EPS) * scale))
        k_all.append(kh * lax.rsqrt(jnp.sum(kh * kh, axis=-1, keepdims=True) + EPS))
        v_all.append(qkv[:, 2 * gdn_width + lo:2 * gdn_width + lo + HEAD_DIM])

    operands = _gdn_chunk_operands(q_all, k_all, v_all, gb)

    lru_ref[...] = _lru_branch(y[:, 3 * gdn_width:] + lcb_ref[...], pg[:, gdn_width:],
                               wlru_ref, blru_ref, ap_ref, hcarry).astype(lru_ref.dtype)

    outs = _gdn_recurrence(operands, state)
    for hd in range(n_heads):
        lo = hd * HEAD_DIM
        z = pg[:, lo:lo + HEAD_DIM]
        o = _rmsnorm(outs[hd], gnw_ref[...]) * (z * _sigmoid(z))
        gdn_ref[:, lo:lo + HEAD_DIM] = o.astype(gdn_ref.dtype)


def _mixer(x, nw, wconv, wgate, wba, cw, lcb, wlru, blru, ap, alog, dtb, gnw, *, ts):
    bsz, seq, d_model = x.shape
    lru_width = lcb.shape[1]
    gdn_width = wgate.shape[1] - lru_width
    n_heads = gdn_width // HEAD_DIM
    tile = lambda w: pl.BlockSpec((None, ts, w), lambda b, s: (b, s, 0))
    consts = (nw, wconv, wgate, wba, cw, lcb, wlru, blru, ap, alog, dtb, gnw)
    return pl.pallas_call(
        _mixer_kernel,
        out_shape=(jax.ShapeDtypeStruct((bsz, seq, gdn_width), _BF16),
                   jax.ShapeDtypeStruct((bsz, seq, lru_width), _BF16)),
        grid=(bsz, seq // ts),
        in_specs=[tile(d_model)] + [_const_spec(c.shape) for c in consts],
        out_specs=(tile(gdn_width), tile(lru_width)),
        scratch_shapes=[pltpu.VMEM((ts + CONV_HALO, wconv.shape[1]), _F32),
                        pltpu.VMEM((1, lru_width), _F32),
                        pltpu.VMEM((n_heads, HEAD_DIM, HEAD_DIM), _F32)],
        compiler_params=pltpu.CompilerParams(
            dimension_semantics=("parallel", "arbitrary"),
            vmem_limit_bytes=VMEM_LIMIT_BYTES),
        name="mixer",
    )(x, *consts)


def _mlp_kernel(x_ref, gdn_ref, lru_ref, wog_ref, wol_ref, nw_ref, w1_ref, w2_ref, fnw_ref,
                o_ref, *, ff_chunk):
    d_ff = w1_ref.shape[1]
    x1 = (x_ref[...]
          + jnp.dot(gdn_ref[...], wog_ref[...], preferred_element_type=_F32)
          + jnp.dot(lru_ref[...], wol_ref[...], preferred_element_type=_F32))
    mb = _rmsnorm(x1, nw_ref[...]).astype(_BF16)
    ff = None
    for c in range(d_ff // ff_chunk):
        c0 = c * ff_chunk
        u = jnp.maximum(jnp.dot(mb, w1_ref[:, c0:c0 + ff_chunk], preferred_element_type=_F32), 0.0)
        t = jnp.dot((u * u).astype(_BF16), w2_ref[c0:c0 + ff_chunk, :],
                    preferred_element_type=_F32)
        ff = t if ff is None else ff + t
    o_ref[...] = _rmsnorm(x1 + ff, fnw_ref[...])


def _mlp(x, gdn, lru, wog, wol, nw, w1, w2, fnw, *, tm, ff_chunk):
    n_tok, d_model = x.shape
    tile = lambda w: pl.BlockSpec((tm, w), lambda i: (i, 0))
    consts = (wog, wol, nw, w1, w2, fnw)
    return pl.pallas_call(
        functools.partial(_mlp_kernel, ff_chunk=ff_chunk),
        out_shape=jax.ShapeDtypeStruct((n_tok, d_model), _F32),
        grid=(n_tok // tm,),
        in_specs=[tile(d_model), tile(gdn.shape[1]), tile(lru.shape[1])]
                 + [_const_spec(c.shape) for c in consts],
        out_specs=tile(d_model),
        compiler_params=pltpu.CompilerParams(
            dimension_semantics=("parallel",),
            vmem_limit_bytes=VMEM_LIMIT_BYTES),
        name="outproj_mlp",
    )(x, gdn, lru, *consts)


def _block_diag(w):
    g, i, j = w.shape
    eye = jnp.eye(g, dtype=w.dtype)
    return (eye[:, None, :, None] * w[:, :, None, :]).reshape(g * i, g * j)


def _layer(x, norm_mix_w, w_in, gdn_conv_w, gdn_A_log, gdn_dt_bias, gdn_norm_w,
           lru_conv_w, lru_conv_b, lru_gate_a_w, lru_gate_a_b, lru_gate_x_w, lru_gate_x_b,
           lru_a_param, w_out, norm_mlp_w, w_ff1, w_ff2, final_norm_w, *, seq_tile, tok_tile,
           ff_chunk):
    bsz, seq, d_model = x.shape
    n_heads = gdn_A_log.shape[0]
    gdn_width = n_heads * HEAD_DIM
    lru_width = lru_conv_b.shape[0]
    assert 2 * n_heads <= LANES
    assert seq % seq_tile == 0 and seq_tile % GDN_CHUNK == 0
    assert (bsz * seq) % tok_tile == 0 and w_ff1.shape[1] % ff_chunk == 0

    o_qkv, o_z = 0, 3 * gdn_width
    o_b = o_z + gdn_width
    o_a = o_b + n_heads
    o_lx = o_a + n_heads
    o_lg = o_lx + lru_width
    wconv = jnp.concatenate([w_in[:, o_qkv:o_z], w_in[:, o_lx:o_lg]], axis=1).astype(_BF16)
    wgate = jnp.concatenate([w_in[:, o_z:o_b], w_in[:, o_lg:o_lg + lru_width]], axis=1).astype(_BF16)
    wba = jnp.pad(w_in[:, o_b:o_lx], ((0, 0), (0, LANES - 2 * n_heads))).astype(_BF16)
    cw = jnp.concatenate([gdn_conv_w, lru_conv_w], axis=1).astype(_F32)
    pad_heads = lambda p: jnp.pad(p.astype(_F32), (n_heads, LANES - 2 * n_heads)).reshape(1, LANES)
    wlru = jnp.concatenate([_block_diag(lru_gate_a_w), _block_diag(lru_gate_x_w)], axis=1).astype(_BF16)
    blru = jnp.concatenate([lru_gate_a_b.reshape(1, -1), lru_gate_x_b.reshape(1, -1)], axis=1)

    gdn, lru = _mixer(
        x, norm_mix_w.reshape(1, -1), wconv, wgate, wba, cw, lru_conv_b.reshape(1, -1),
        wlru, blru, lru_a_param.reshape(1, -1), pad_heads(gdn_A_log), pad_heads(gdn_dt_bias),
        gdn_norm_w.reshape(1, -1), ts=seq_tile)

    n_tok = bsz * seq
    out = _mlp(x.reshape(n_tok, d_model), gdn.reshape(n_tok, gdn_width),
               lru.reshape(n_tok, lru_width),
               w_out[:gdn_width].astype(_BF16), w_out[gdn_width:].astype(_BF16),
               norm_mlp_w.reshape(1, -1), w_ff1.astype(_BF16), w_ff2.astype(_BF16),
               final_norm_w.reshape(1, -1), tm=tok_tile, ff_chunk=ff_chunk)
    return out.reshape(bsz, seq, d_model)


def kernel(x, norm_mix_w, w_in, gdn_conv_w, gdn_A_log, gdn_dt_bias, gdn_norm_w, lru_conv_w, lru_conv_b, lru_gate_a_w, lru_gate_a_b, lru_gate_x_w, lru_gate_x_b, lru_a_param, w_out, norm_mlp_w, w_ff1, w_ff2, final_norm_w):
    assert norm_mix_w.shape[0] == 1, "single-layer stack"
    return _layer(x, norm_mix_w[0], w_in[0], gdn_conv_w[0], gdn_A_log[0], gdn_dt_bias[0],
                  gdn_norm_w[0], lru_conv_w[0], lru_conv_b[0], lru_gate_a_w[0], lru_gate_a_b[0],
                  lru_gate_x_w[0], lru_gate_x_b[0], lru_a_param[0], w_out[0], norm_mlp_w[0],
                  w_ff1[0], w_ff2[0], final_norm_w, seq_tile=256, tok_tile=512, ff_chunk=1024)
```

```python
import functools
import math

import jax
import jax.numpy as jnp
from jax import lax
from jax.experimental import pallas as pl
from jax.experimental.pallas import tpu as pltpu

EPS = 1e-6
HEAD_DIM = 128
LANES = 128
SUBLANES = 8
GDN_CHUNK = 64
INV_BASE = 16
LRU_C = 8.0
CONV_WIDTH = 4
CONV_HALO = 8
PROJ_COLS = 512
LRU_ROWS = 128
MLP_ROWS = 512
F32_TINY = 1.1754944e-38
MXU_TILE = 256
VMEM_LIMIT_BYTES = 56 * 1024 * 1024

_BF16 = jnp.bfloat16
_F32 = jnp.float32
_NT_DIMS = (((1,), (1,)), ((), ()))


def _dot(a, b):
    return jnp.dot(a.astype(_BF16), b.astype(_BF16), preferred_element_type=_F32)


def _dot_nt(a, b):
    return lax.dot_general(a.astype(_BF16), b.astype(_BF16), _NT_DIMS,
                           preferred_element_type=_F32)


def _sigmoid(x):
    return 1.0 / (1.0 + jnp.exp(-x))


def _softplus(x):
    return jnp.maximum(x, 0.0) + jnp.log1p(jnp.exp(-jnp.abs(x)))


def _gelu_tanh(x):
    c = math.sqrt(2.0 / math.pi)
    return 0.5 * x * (1.0 + jnp.tanh(c * (x + 0.044715 * (x * x * x))))


def _rmsnorm(x, w):
    return x * lax.rsqrt(jnp.mean(x * x, axis=-1, keepdims=True) + EPS) * w


def _const_spec(shape):
    nd = len(shape)
    return pl.BlockSpec(shape, lambda *_: (0,) * nd, pipeline_mode=pl.Buffered(1))


def _causal_conv(pbuf, cw_ref, ts):
    pe = pbuf[...]
    pd = pltpu.roll(pe, 1, 0)
    near = cw_ref[3:4, :] * pe + cw_ref[2:3, :] * pd
    far = cw_ref[1:2, :] * pe + cw_ref[0:1, :] * pd
    y = near + pltpu.roll(far, 2, 0)
    return y[CONV_HALO:CONV_HALO + ts, :]


def _interleave(*stages):
    live = list(stages)
    while live:
        for entry in list(live):
            stage, pieces = entry
            for _ in range(pieces):
                try:
                    next(stage)
                except StopIteration:
                    live.remove(entry)
                    break


def _block_diag_dot(xb, w_ref, col0):
    k = xb.shape[1]
    return jnp.concatenate(
        [jnp.dot(xb[:, t:t + MXU_TILE], w_ref[t:t + MXU_TILE, col0 + t:col0 + t + MXU_TILE],
                 preferred_element_type=_F32) for t in range(0, k, MXU_TILE)], axis=1)


def _lru_stage(xr_ref, gate_ref, wlru_ref, blru_ref, ap_ref, hcarry, lru_ref):
    ts, lru_width = xr_ref.shape
    nsp = _softplus(-ap_ref[...])
    row = lax.broadcasted_iota(jnp.int32, (LRU_ROWS, lru_width), 0) % SUBLANES
    carry = hcarry[...]
    for b0 in range(0, ts, LRU_ROWS):
        xb = xr_ref[b0:b0 + LRU_ROWS, :]
        xr = xb.astype(_F32)
        r = _sigmoid(_block_diag_dot(xb, wlru_ref, 0) + blru_ref[:, :lru_width])
        log_a = -LRU_C * r * nsp
        a = jnp.exp(log_a)
        yield
        i = _sigmoid(_block_diag_dot(xb, wlru_ref, lru_width) + blru_ref[:, lru_width:])
        v = jnp.maximum(1.0 - jnp.exp(2.0 * log_a), 0.0)
        b = v * lax.rsqrt(jnp.maximum(v, F32_TINY)) * (i * xr)
        yield
        d = 1
        while d < SUBLANES:
            keep = row >= d
            a_sh = jnp.where(keep, pltpu.roll(a, d, 0), 1.0)
            b_sh = jnp.where(keep, pltpu.roll(b, d, 0), 0.0)
            b = a * b_sh + b
            a = a * a_sh
            d *= 2
        yield
        groups = []
        for r0 in range(0, LRU_ROWS, SUBLANES):
            hg = b[r0:r0 + SUBLANES] + a[r0:r0 + SUBLANES] * carry
            carry = hg[SUBLANES - 1:SUBLANES]
            groups.append(hg)
        gate = gate_ref[b0:b0 + LRU_ROWS, :].astype(_F32)
        lru_ref[b0:b0 + LRU_ROWS, :] = (jnp.concatenate(groups, axis=0)
                                        * _gelu_tanh(gate)).astype(lru_ref.dtype)
        yield
    hcarry[...] = carry


def _unit_lower_inverse(a_heads, row, col):
    eye = (row == col).astype(_F32)
    base = (row // INV_BASE) == (col // INV_BASE)
    p = [jnp.where(base, a, 0.0) for a in a_heads]
    x = [eye - d for d in p]
    span = 2
    while span < INV_BASE:
        p = [_dot(ph, ph) for ph in p]
        x = [xh + _dot(xh, ph) for xh, ph in zip(x, p)]
        span *= 2
    size = INV_BASE
    while size < GDN_CHUNK:
        off = ((row // (2 * size)) == (col // (2 * size))) & ((row // size) != (col // size))
        xe = [_dot(xh, jnp.where(off, a, 0.0)) for xh, a in zip(x, a_heads)]
        x = [xh - _dot(xeh, xh) for xh, xeh in zip(x, xe)]
        size *= 2
    return x


def _gdn_chunk_operands(q_all, k_all, v_all, gb, lhs_s, n_s, o2_s, gam_s):
    n_heads = len(q_all)
    heads = range(n_heads)
    ts = gb.shape[0]
    n_chunks = ts // GDN_CHUNK
    row = lax.broadcasted_iota(jnp.int32, (ts, ts), 0)
    col = lax.broadcasted_iota(jnp.int32, (ts, ts), 1)
    same_chunk = (row // GDN_CHUNK) == (col // GDN_CHUNK)
    causal = same_chunk & (row >= col)
    strict = same_chunk & (row > col)

    ones_tri = causal.astype(_BF16)
    gb_hi = gb.astype(_BF16)
    gb_r1 = gb - gb_hi.astype(_F32)
    gb_mid = gb_r1.astype(_BF16)
    gb_lo = (gb_r1 - gb_mid.astype(_F32)).astype(_BF16)
    gcum = (jnp.dot(ones_tri, gb_hi, preferred_element_type=_F32)
            + jnp.dot(ones_tri, gb_mid, preferred_element_type=_F32)
            + jnp.dot(ones_tri, gb_lo, preferred_element_type=_F32))
    gcum_t = gcum.T

    gcol_all, a_all, rhs_all, qk_all, qg_all = [], [], [], [], []
    for hd in heads:
        beta = gb[:, hd:hd + 1]
        gcol = gcum[:, n_heads + hd:n_heads + hd + 1]
        grow = gcum_t[n_heads + hd:n_heads + hd + 1, :]
        decay = jnp.where(causal, jnp.exp(jnp.where(causal, gcol - grow, 0.0)), 0.0)
        q, k, v = q_all[hd], k_all[hd], v_all[hd]
        egc = jnp.exp(gcol)
        kb = k * beta
        gcol_all.append(gcol)
        a_all.append(jnp.where(strict, _dot_nt(kb, k) * decay, 0.0))
        rhs_all.append(jnp.concatenate([v * beta, kb * egc], axis=1))
        qk_all.append(_dot_nt(q, k) * decay)
        qg_all.append(q * egc)
    t_all = _unit_lower_inverse(a_all, row, col)
    sol_all = [_dot(t, rhs) for t, rhs in zip(t_all, rhs_all)]

    for c in range(n_chunks):
        r0 = c * GDN_CHUNK
        r1 = r0 + GDN_CHUNK
        for hd in heads:
            idx = hd * n_chunks + c
            sol = sol_all[hd][r0:r1]
            g_last = gcol_all[hd][r1 - 1:r1, :]
            k_dec = k_all[hd][r0:r1] * jnp.exp(g_last - gcol_all[hd][r0:r1])
            kd_sol = _dot(k_dec.T, sol)
            qk_sol = _dot(qk_all[hd][r0:r1, r0:r1], sol)
            lhs_s[idx] = jnp.concatenate(
                [kd_sol[:, HEAD_DIM:], qg_all[hd][r0:r1] - qk_sol[:, HEAD_DIM:]],
                axis=0).astype(lhs_s.dtype)
            n_s[idx] = kd_sol[:, :HEAD_DIM]
            o2_s[idx] = qk_sol[:, :HEAD_DIM]
            gam_s[idx] = jnp.broadcast_to(jnp.exp(g_last), gam_s.shape[1:])


def _gdn_recurrence_stage(lhs_s, n_s, o2_s, gam_s, z_ref, gnw_ref, state, gdn_ref):
    n_heads = state.shape[0]
    heads = range(n_heads)
    n_chunks = lhs_s.shape[0] // n_heads
    s_all = [state[hd] for hd in heads]
    outs = [[] for _ in heads]
    for c in range(n_chunks):
        for hd in heads:
            idx = hd * n_chunks + c
            s = s_all[hd]
            r = jnp.dot(lhs_s[idx], s.astype(_BF16), preferred_element_type=_F32)
            outs[hd].append(r[HEAD_DIM:] + o2_s[idx])
            s_all[hd] = s * gam_s[idx][0:1, :] - r[:HEAD_DIM] + n_s[idx]
        yield
    for hd in heads:
        lo = hd * HEAD_DIM
        state[hd] = s_all[hd]
        z = z_ref[:, lo:lo + HEAD_DIM]
        o = _rmsnorm(jnp.concatenate(outs[hd], axis=0), gnw_ref[...]) * (z * _sigmoid(z))
        gdn_ref[:, lo:lo + HEAD_DIM] = o.astype(gdn_ref.dtype)
        yield


def _project_stage(x_ref, nw_ref, wconv_ref, wgate_ref, wba_ref, pbuf, out):
    ts = x_ref.shape[0]
    hb = _rmsnorm(x_ref[...], nw_ref[...]).astype(_BF16)
    yield
    for c0 in range(0, wconv_ref.shape[1], PROJ_COLS):
        pbuf[CONV_HALO:CONV_HALO + ts, c0:c0 + PROJ_COLS] = jnp.dot(
            hb, wconv_ref[:, c0:c0 + PROJ_COLS], preferred_element_type=_F32)
        yield
    parts = []
    for c0 in range(0, wgate_ref.shape[1], PROJ_COLS):
        parts.append(jnp.dot(hb, wgate_ref[:, c0:c0 + PROJ_COLS], preferred_element_type=_F32))
        yield
    out["pg"] = jnp.concatenate(parts, axis=1)
    out["pba"] = jnp.dot(hb, wba_ref[...], preferred_element_type=_F32)


def _mixer_kernel(x_ref, nw_ref, wconv_ref, wgate_ref, wba_ref, cw_ref, lcb_ref,
                  alog_ref, dtb_ref, gnw_ref,
                  gdn_ref, xr_ref, gate_ref,
                  pbuf, z_s, lhs_s, n_s, o2_s, gam_s, state, *, tiles_per_seq, n_tiles):
    j = pl.program_id(0)
    ts = x_ref.shape[0]
    gdn_width = gdn_ref.shape[1]
    n_heads = gdn_width // HEAD_DIM

    @pl.when(j == 0)
    def _():
        for ref in (z_s, lhs_s, n_s, o2_s, gam_s):
            ref[...] = jnp.zeros_like(ref)

    @pl.when(j % tiles_per_seq == 0)
    def _():
        pbuf[0:CONV_HALO, :] = jnp.zeros((CONV_HALO, pbuf.shape[1]), _F32)

    @pl.when((j == 0) | (j % tiles_per_seq == 1 % tiles_per_seq))
    def _():
        state[...] = jnp.zeros_like(state)

    proj = {}
    _interleave(
        (_project_stage(x_ref, nw_ref, wconv_ref, wgate_ref, wba_ref, pbuf, proj), 1),
        (_gdn_recurrence_stage(lhs_s, n_s, o2_s, gam_s, z_s, gnw_ref, state, gdn_ref), 1))

    y = _causal_conv(pbuf, cw_ref, ts)
    pbuf[0:CONV_HALO, :] = pbuf[ts:ts + CONV_HALO, :]

    pba = proj["pba"]
    lane = lax.broadcasted_iota(jnp.int32, pba.shape, 1)
    gb = jnp.where(lane < n_heads, _sigmoid(pba),
                   -jnp.exp(alog_ref[...]) * _softplus(pba + dtb_ref[...]))

    qkv = y[:, :3 * gdn_width]
    qkv = qkv * _sigmoid(qkv)
    scale = HEAD_DIM ** -0.5
    q_all, k_all, v_all = [], [], []
    for hd in range(n_heads):
        lo = hd * HEAD_DIM
        qh = qkv[:, lo:lo + HEAD_DIM]
        kh = qkv[:, gdn_width + lo:gdn_width + lo + HEAD_DIM]
        q_all.append(qh * (lax.rsqrt(jnp.sum(qh * qh, axis=-1, keepdims=True) + EPS) * scale))
        k_all.append(kh * lax.rsqrt(jnp.sum(kh * kh, axis=-1, keepdims=True) + EPS))
        v_all.append(qkv[:, 2 * gdn_width + lo:2 * gdn_width + lo + HEAD_DIM])

    _gdn_chunk_operands(q_all, k_all, v_all, gb, lhs_s, n_s, o2_s, gam_s)
    pg = proj["pg"]
    z_s[...] = pg[:, :gdn_width]

    @pl.when(j < n_tiles)
    def _():
        xr_ref[...] = (y[:, 3 * gdn_width:] + lcb_ref[...]).astype(xr_ref.dtype)
        gate_ref[...] = pg[:, gdn_width:].astype(gate_ref.dtype)


def _mixer(x, nw, wconv, wgate, wba, cw, lcb, alog, dtb, gnw, *, ts, seq):
    n_tok, d_model = x.shape
    lru_width = lcb.shape[1]
    gdn_width = wgate.shape[1] - lru_width
    n_heads = gdn_width // HEAD_DIM
    n_tiles = n_tok // ts
    n_ops = n_heads * (ts // GDN_CHUNK)
    consts = (nw, wconv, wgate, wba, cw, lcb, alog, dtb, gnw)
    front_tile = lambda w: pl.BlockSpec((ts, w), lambda j: (jnp.minimum(j, n_tiles - 1), 0))
    back_tile = lambda w: pl.BlockSpec((ts, w), lambda j: (jnp.maximum(j - 1, 0), 0))
    return pl.pallas_call(
        functools.partial(_mixer_kernel, tiles_per_seq=seq // ts, n_tiles=n_tiles),
        out_shape=(jax.ShapeDtypeStruct((n_tok, gdn_width), _BF16),
                   jax.ShapeDtypeStruct((n_tok, lru_width), _BF16),
                   jax.ShapeDtypeStruct((n_tok, lru_width), _BF16)),
        grid=(n_tiles + 1,),
        in_specs=[front_tile(d_model)] + [_const_spec(c.shape) for c in consts],
        out_specs=(back_tile(gdn_width), front_tile(lru_width), front_tile(lru_width)),
        scratch_shapes=[pltpu.VMEM((ts + CONV_HALO, wconv.shape[1]), _F32),
                        pltpu.VMEM((ts, gdn_width), _F32),
                        pltpu.VMEM((n_ops, HEAD_DIM + GDN_CHUNK, HEAD_DIM), _BF16),
                        pltpu.VMEM((n_ops, HEAD_DIM, HEAD_DIM), _F32),
                        pltpu.VMEM((n_ops, GDN_CHUNK, HEAD_DIM), _F32),
                        pltpu.VMEM((n_ops, SUBLANES, LANES), _F32),
                        pltpu.VMEM((n_heads, HEAD_DIM, HEAD_DIM), _F32)],
        compiler_params=pltpu.CompilerParams(
            dimension_semantics=("arbitrary",),
            vmem_limit_bytes=VMEM_LIMIT_BYTES),
        name="mixer",
    )(x, *consts)


def _mlp_stage(x_ref, gdn_ref, lru_s, wog_ref, wol_ref, nw_ref, w1_ref, w2_ref, fnw_ref, o_ref,
               ff_chunk):
    d_ff = w1_ref.shape[1]
    for r0 in range(0, x_ref.shape[0], MLP_ROWS):
        rows = slice(r0, r0 + MLP_ROWS)
        x1 = (x_ref[rows, :]
              + jnp.dot(gdn_ref[rows, :], wog_ref[...], preferred_element_type=_F32)
              + jnp.dot(lru_s[rows, :], wol_ref[...], preferred_element_type=_F32))
        yield
        mb = _rmsnorm(x1, nw_ref[...]).astype(_BF16)
        ff = None
        for c in range(d_ff // ff_chunk):
            c0 = c * ff_chunk
            u = jnp.maximum(
                jnp.dot(mb, w1_ref[:, c0:c0 + ff_chunk], preferred_element_type=_F32), 0.0)
            yield
            t = jnp.dot((u * u).astype(_BF16), w2_ref[c0:c0 + ff_chunk, :],
                        preferred_element_type=_F32)
            ff = t if ff is None else ff + t
            yield
        o_ref[rows, :] = _rmsnorm(x1 + ff, fnw_ref[...])


def _mlp_kernel(x_ref, gdn_ref, xr_ref, gate_ref, wog_ref, wol_ref, nw_ref, w1_ref, w2_ref,
                fnw_ref, wlru_ref, blru_ref, ap_ref, o_ref, lru_s, lru_next, hcarry, *, ff_chunk,
                tiles_per_seq):
    i = pl.program_id(0)

    @pl.when(i == 0)
    def _():
        lru_s[...] = jnp.zeros_like(lru_s)

    @pl.when(i % tiles_per_seq == 0)
    def _():
        hcarry[...] = jnp.zeros_like(hcarry)

    _interleave(
        (_mlp_stage(x_ref, gdn_ref, lru_s, wog_ref, wol_ref, nw_ref, w1_ref, w2_ref, fnw_ref,
                    o_ref, ff_chunk), 1),
        (_lru_stage(xr_ref, gate_ref, wlru_ref, blru_ref, ap_ref, hcarry, lru_next), 1))
    lru_s[...] = lru_next[...]


def _mlp(x, gdn, xr, gate, wog, wol, nw, w1, w2, fnw, wlru, blru, ap, *, tm, ff_chunk, seq):
    n_tok, d_model = x.shape
    n_tiles = n_tok // tm
    lru_width = xr.shape[1]
    front_tile = lambda w: pl.BlockSpec((tm, w), lambda i: (jnp.minimum(i, n_tiles - 1), 0))
    back_tile = lambda w: pl.BlockSpec((tm, w), lambda i: (jnp.maximum(i - 1, 0), 0))
    consts = (wog, wol, nw, w1, w2, fnw, wlru, blru, ap)
    return pl.pallas_call(
        functools.partial(_mlp_kernel, ff_chunk=ff_chunk, tiles_per_seq=seq // tm),
        out_shape=jax.ShapeDtypeStruct((n_tok, d_model), _F32),
        grid=(n_tiles + 1,),
        in_specs=[back_tile(d_model), back_tile(gdn.shape[1]), front_tile(lru_width),
                  front_tile(lru_width)] + [_const_spec(c.shape) for c in consts],
        out_specs=back_tile(d_model),
        scratch_shapes=[pltpu.VMEM((tm, lru_width), _BF16),
                        pltpu.VMEM((tm, lru_width), _BF16),
                        pltpu.VMEM((1, lru_width), _F32)],
        compiler_params=pltpu.CompilerParams(
            dimension_semantics=("arbitrary",),
            vmem_limit_bytes=VMEM_LIMIT_BYTES),
        name="outproj_mlp",
    )(x, gdn, xr, gate, *consts)


def _block_diag(w):
    g, i, j = w.shape
    eye = jnp.eye(g, dtype=w.dtype)
    return (eye[:, None, :, None] * w[:, :, None, :]).reshape(g * i, g * j)


def _layer(x, norm_mix_w, w_in, gdn_conv_w, gdn_A_log, gdn_dt_bias, gdn_norm_w,
           lru_conv_w, lru_conv_b, lru_gate_a_w, lru_gate_a_b, lru_gate_x_w, lru_gate_x_b,
           lru_a_param, w_out, norm_mlp_w, w_ff1, w_ff2, final_norm_w, *, seq_tile, tok_tile,
           ff_chunk):
    bsz, seq, d_model = x.shape
    n_heads = gdn_A_log.shape[0]
    gdn_width = n_heads * HEAD_DIM
    lru_width = lru_conv_b.shape[0]
    assert 2 * n_heads <= LANES
    assert seq % seq_tile == 0 and seq_tile % GDN_CHUNK == 0
    assert seq % tok_tile == 0 and w_ff1.shape[1] % ff_chunk == 0

    o_qkv, o_z = 0, 3 * gdn_width
    o_b = o_z + gdn_width
    o_a = o_b + n_heads
    o_lx = o_a + n_heads
    o_lg = o_lx + lru_width
    wconv = jnp.concatenate([w_in[:, o_qkv:o_z], w_in[:, o_lx:o_lg]], axis=1).astype(_BF16)
    wgate = jnp.concatenate([w_in[:, o_z:o_b], w_in[:, o_lg:o_lg + lru_width]], axis=1).astype(_BF16)
    wba = jnp.pad(w_in[:, o_b:o_lx], ((0, 0), (0, LANES - 2 * n_heads))).astype(_BF16)
    cw = jnp.concatenate([gdn_conv_w, lru_conv_w], axis=1).astype(_F32)
    pad_heads = lambda p: jnp.pad(p.astype(_F32), (n_heads, LANES - 2 * n_heads)).reshape(1, LANES)
    wlru = jnp.concatenate([_block_diag(lru_gate_a_w), _block_diag(lru_gate_x_w)], axis=1).astype(_BF16)
    blru = jnp.concatenate([lru_gate_a_b.reshape(1, -1), lru_gate_x_b.reshape(1, -1)], axis=1)

    n_tok = bsz * seq
    x2d = x.reshape(n_tok, d_model)
    gdn, xr, gate = _mixer(
        x2d, norm_mix_w.reshape(1, -1), wconv, wgate, wba, cw, lru_conv_b.reshape(1, -1),
        pad_heads(gdn_A_log), pad_heads(gdn_dt_bias), gdn_norm_w.reshape(1, -1),
        ts=seq_tile, seq=seq)

    out = _mlp(x2d, gdn, xr, gate,
               w_out[:gdn_width].astype(_BF16), w_out[gdn_width:].astype(_BF16),
               norm_mlp_w.reshape(1, -1), w_ff1.astype(_BF16), w_ff2.astype(_BF16),
               final_norm_w.reshape(1, -1), wlru, blru, lru_a_param.reshape(1, -1),
               tm=tok_tile, ff_chunk=ff_chunk, seq=seq)
    return out.reshape(bsz, seq, d_model)


def kernel(x, norm_mix_w, w_in, gdn_conv_w, gdn_A_log, gdn_dt_bias, gdn_norm_w, lru_conv_w, lru_conv_b, lru_gate_a_w, lru_gate_a_b, lru_gate_x_w, lru_gate_x_b, lru_a_param, w_out, norm_mlp_w, w_ff1, w_ff2, final_norm_w):
    assert norm_mix_w.shape[0] == 1, "single-layer stack"
    return _layer(x, norm_mix_w[0], w_in[0], gdn_conv_w[0], gdn_A_log[0], gdn_dt_bias[0],
                  gdn_norm_w[0], lru_conv_w[0], lru_conv_b[0], lru_gate_a_w[0], lru_gate_a_b[0],
                  lru_gate_x_w[0], lru_gate_x_b[0], lru_a_param[0], w_out[0], norm_mlp_w[0],
                  w_ff1[0], w_ff2[0], final_norm_w, seq_tile=256, tok_tile=512, ff_chunk=512)
```

```python
import functools
import math

import jax
import jax.numpy as jnp
from jax import lax
from jax.experimental import pallas as pl
from jax.experimental.pallas import tpu as pltpu

EPS = 1e-6
HEAD_DIM = 128
LANES = 128
SUBLANES = 8
GDN_CHUNK = 64
INV_BASE = 16
LRU_C = 8.0
CONV_WIDTH = 4
CONV_HALO = 8
PROJ_COLS = 512
CONV_COLS = 256
LRU_ROWS = 128
MLP_ROWS = 512
F32_TINY = 1.1754944e-38
MXU_TILE = 256
VMEM_LIMIT_BYTES = 56 * 1024 * 1024

_BF16 = jnp.bfloat16
_F32 = jnp.float32
_NT_DIMS = (((1,), (1,)), ((), ()))


def _dot(a, b):
    return jnp.dot(a.astype(_BF16), b.astype(_BF16), preferred_element_type=_F32)


def _dot_nt(a, b):
    return lax.dot_general(a.astype(_BF16), b.astype(_BF16), _NT_DIMS,
                           preferred_element_type=_F32)


def _sigmoid(x):
    return 1.0 / (1.0 + jnp.exp(-x))


def _softplus(x):
    return jnp.maximum(x, 0.0) + jnp.log1p(jnp.exp(-jnp.abs(x)))


def _gelu_tanh(x):
    c = math.sqrt(2.0 / math.pi)
    return 0.5 * x * (1.0 + jnp.tanh(c * (x + 0.044715 * (x * x * x))))


def _rmsnorm(x, w):
    return x * lax.rsqrt(jnp.mean(x * x, axis=-1, keepdims=True) + EPS) * w


def _const_spec(shape):
    nd = len(shape)
    return pl.BlockSpec(shape, lambda *_: (0,) * nd, pipeline_mode=pl.Buffered(1))


def _causal_conv(pbuf, cw_ref, ts, c0, c1):
    pe = pbuf[:, c0:c1]
    pd = pltpu.roll(pe, 1, 0)
    near = cw_ref[3:4, c0:c1] * pe + cw_ref[2:3, c0:c1] * pd
    far = cw_ref[1:2, c0:c1] * pe + cw_ref[0:1, c0:c1] * pd
    y = near + pltpu.roll(far, 2, 0)
    return y[CONV_HALO:CONV_HALO + ts, :]


def _interleave(*stages):
    live = list(stages)
    while live:
        for entry in list(live):
            stage, pieces = entry
            for _ in range(pieces):
                try:
                    next(stage)
                except StopIteration:
                    live.remove(entry)
                    break


def _block_diag_dot(xb, w_ref, col0):
    k = xb.shape[1]
    return jnp.concatenate(
        [jnp.dot(xb[:, t:t + MXU_TILE], w_ref[t:t + MXU_TILE, col0 + t:col0 + t + MXU_TILE],
                 preferred_element_type=_F32) for t in range(0, k, MXU_TILE)], axis=1)


def _lru_stage(xr_ref, gate_ref, wlru_ref, blru_ref, ap_ref, hcarry, lru_ref):
    ts, lru_width = xr_ref.shape
    nsp = _softplus(-ap_ref[...])
    row = lax.broadcasted_iota(jnp.int32, (LRU_ROWS, lru_width), 0) % SUBLANES
    carry = hcarry[...]
    for b0 in range(0, ts, LRU_ROWS):
        xb = xr_ref[b0:b0 + LRU_ROWS, :]
        xr = xb.astype(_F32)
        r = _sigmoid(_block_diag_dot(xb, wlru_ref, 0) + blru_ref[:, :lru_width])
        log_a = -LRU_C * r * nsp
        a = jnp.exp(log_a)
        yield
        i = _sigmoid(_block_diag_dot(xb, wlru_ref, lru_width) + blru_ref[:, lru_width:])
        v = jnp.maximum(1.0 - jnp.exp(2.0 * log_a), 0.0)
        b = v * lax.rsqrt(jnp.maximum(v, F32_TINY)) * (i * xr)
        yield
        d = 1
        while d < SUBLANES:
            keep = row >= d
            a_sh = jnp.where(keep, pltpu.roll(a, d, 0), 1.0)
            b_sh = jnp.where(keep, pltpu.roll(b, d, 0), 0.0)
            b = a * b_sh + b
            a = a * a_sh
            d *= 2
        yield
        groups = []
        for r0 in range(0, LRU_ROWS, SUBLANES):
            hg = b[r0:r0 + SUBLANES] + a[r0:r0 + SUBLANES] * carry
            carry = hg[SUBLANES - 1:SUBLANES]
            groups.append(hg)
        gate = gate_ref[b0:b0 + LRU_ROWS, :].astype(_F32)
        lru_ref[b0:b0 + LRU_ROWS, :] = (jnp.concatenate(groups, axis=0)
                                        * _gelu_tanh(gate)).astype(lru_ref.dtype)
        yield
    hcarry[...] = carry


def _pack_chunks(m, lane_chunk):
    n = m.shape[0] // GDN_CHUNK
    out = m[(n - 1) * GDN_CHUNK:, :]
    for c in range(n - 2, -1, -1):
        out = jnp.where(lane_chunk == c, m[c * GDN_CHUNK:(c + 1) * GDN_CHUNK, :], out)
    return out


def _block_diag_of(p, bd_mask):
    pb = p.astype(_BF16)
    return jnp.concatenate([pb] * (bd_mask.shape[0] // GDN_CHUNK), axis=0) * bd_mask


def _packed_dot(x, p_bd):
    return jnp.dot(x.astype(_BF16), p_bd, preferred_element_type=_F32)


def _unit_lower_inverse(a_heads, r, j, bd_mask):
    c = GDN_CHUNK
    eye = (r == j).astype(_F32)
    base = (r // INV_BASE) == (j // INV_BASE)
    d = [jnp.where(base, a, 0.0) for a in a_heads]
    x = [eye - dh for dh in d]
    p = [_packed_dot(dh, _block_diag_of(dh, bd_mask)) for dh in d]
    yield
    span = 4
    while span < INV_BASE:
        both = [_packed_dot(jnp.concatenate([xh, ph], axis=0), _block_diag_of(ph, bd_mask))
                for xh, ph in zip(x, p)]
        x = [xh + bh[:c] for xh, bh in zip(x, both)]
        p = [bh[c:] for bh in both]
        span *= 2
        yield
    x = [xh + _packed_dot(xh, _block_diag_of(ph, bd_mask)) for xh, ph in zip(x, p)]
    yield
    size = INV_BASE
    while size < GDN_CHUNK:
        off = ((r // (2 * size)) == (j // (2 * size))) & ((r // size) != (j // size))
        xe = [_packed_dot(xh, _block_diag_of(jnp.where(off, a, 0.0), bd_mask))
              for xh, a in zip(x, a_heads)]
        yield
        x = [xh - _packed_dot(xeh, _block_diag_of(xh, bd_mask)) for xh, xeh in zip(x, xe)]
        yield
        size *= 2
    return x


def _gdn_operand_stage(pbuf, cw_ref, pba, alog_ref, dtb_ref, gdn_width,
                       lhs_s, n_s, o2_s, gam_s, out):
    ts = pba.shape[0]
    n_heads = gdn_width // HEAD_DIM
    heads = range(n_heads)
    n_chunks = ts // GDN_CHUNK

    conv_parts = []
    for c0 in range(0, pbuf.shape[1], CONV_COLS):
        conv_parts.append(_causal_conv(pbuf, cw_ref, ts, c0, c0 + CONV_COLS))
        yield
    conv_out = jnp.concatenate(conv_parts, axis=1)
    out["lru_x"] = conv_out[:, 3 * gdn_width:]

    lane128 = lax.broadcasted_iota(jnp.int32, pba.shape, 1)
    gb = jnp.where(lane128 < n_heads, _sigmoid(pba),
                   -jnp.exp(alog_ref[...]) * _softplus(pba + dtb_ref[...]))
    scale = HEAD_DIM ** -0.5
    q_all, k_all, v_all = [], [], []
    for hd in heads:
        lo = hd * HEAD_DIM
        qh = conv_out[:, lo:lo + HEAD_DIM]
        qh = qh * _sigmoid(qh)
        q_all.append(qh * (lax.rsqrt(jnp.sum(qh * qh, axis=-1, keepdims=True) + EPS) * scale))
        yield
        kh = conv_out[:, gdn_width + lo:gdn_width + lo + HEAD_DIM]
        kh = kh * _sigmoid(kh)
        k_all.append(kh * lax.rsqrt(jnp.sum(kh * kh, axis=-1, keepdims=True) + EPS))
        yield
        vh = conv_out[:, 2 * gdn_width + lo:2 * gdn_width + lo + HEAD_DIM]
        v_all.append(vh * _sigmoid(vh))
        yield

    row = lax.broadcasted_iota(jnp.int32, (ts, ts), 0)
    col = lax.broadcasted_iota(jnp.int32, (ts, ts), 1)
    same_chunk = (row // GDN_CHUNK) == (col // GDN_CHUNK)
    bd_mask = jnp.where(same_chunk, 1.0, 0.0).astype(_BF16)
    r = lax.broadcasted_iota(jnp.int32, (GDN_CHUNK, ts), 0)
    lane = lax.broadcasted_iota(jnp.int32, (GDN_CHUNK, ts), 1)
    lane_chunk = lane // GDN_CHUNK
    j = lane % GDN_CHUNK
    causal = r >= j
    strict = r > j

    ones_tri = jnp.where(same_chunk & (row >= col), 1.0, 0.0).astype(_BF16)
    gb_hi = gb.astype(_BF16)
    gb_r1 = gb - gb_hi.astype(_F32)
    gb_mid = gb_r1.astype(_BF16)
    gb_lo = (gb_r1 - gb_mid.astype(_F32)).astype(_BF16)
    gcum = (jnp.dot(ones_tri, gb_hi, preferred_element_type=_F32)
            + jnp.dot(ones_tri, gb_mid, preferred_element_type=_F32)
            + jnp.dot(ones_tri, gb_lo, preferred_element_type=_F32))
    gcum_t = gcum.T
    yield

    gcol_all, a_all, rhs_all, qk_all, qg_all = [], [], [], [], []
    for hd in heads:
        beta = gb[:, hd:hd + 1]
        gcol = gcum[:, n_heads + hd:n_heads + hd + 1]
        grow = gcum_t[n_heads + hd:n_heads + hd + 1, :]
        gpk = _pack_chunks(jnp.broadcast_to(gcol, (ts, ts)), lane_chunk)
        decay = jnp.where(causal, jnp.exp(jnp.where(causal, gpk - grow, 0.0)), 0.0)
        q, k, v = q_all[hd], k_all[hd], v_all[hd]
        egc = jnp.exp(gcol)
        kb = k * beta
        gcol_all.append(gcol)
        a_all.append(jnp.where(strict, _pack_chunks(_dot_nt(kb, k), lane_chunk) * decay, 0.0))
        rhs_all.append(jnp.concatenate([v * beta, kb * egc], axis=1))
        qk_all.append(_pack_chunks(_dot_nt(q, k), lane_chunk) * decay)
        qg_all.append(q * egc)
        yield
    t_all = yield from _unit_lower_inverse(a_all, r, j, bd_mask)
    sol_all = [jnp.dot(_block_diag_of(t, bd_mask), rhs.astype(_BF16), preferred_element_type=_F32)
               for t, rhs in zip(t_all, rhs_all)]
    yield

    for c in range(n_chunks):
        r0 = c * GDN_CHUNK
        r1 = r0 + GDN_CHUNK
        for hd in heads:
            idx = hd * n_chunks + c
            sol = sol_all[hd][r0:r1]
            g_last = gcol_all[hd][r1 - 1:r1, :]
            k_dec = k_all[hd][r0:r1] * jnp.exp(g_last - gcol_all[hd][r0:r1])
            kd_sol = _dot(k_dec.T, sol)
            qk_sol = _dot(qk_all[hd][:, r0:r1], sol)
            lhs_s[idx] = jnp.concatenate(
                [kd_sol[:, HEAD_DIM:], qg_all[hd][r0:r1] - qk_sol[:, HEAD_DIM:]],
                axis=0).astype(lhs_s.dtype)
            n_s[idx] = kd_sol[:, :HEAD_DIM]
            o2_s[idx] = qk_sol[:, :HEAD_DIM]
            gam_s[idx] = jnp.broadcast_to(jnp.exp(g_last), gam_s.shape[1:])
        yield


def _gdn_recurrence_stage(lhs_s, n_s, o2_s, gam_s, z_ref, gnw_ref, state, gdn_ref, fresh):
    n_heads = state.shape[0]
    heads = range(n_heads)
    n_chunks = lhs_s.shape[0] // n_heads
    s_all = [jnp.where(fresh, 0.0, state[hd]) for hd in heads]
    outs = [[] for _ in heads]
    for c in range(n_chunks):
        for hd in heads:
            idx = hd * n_chunks + c
            s = s_all[hd]
            r = jnp.dot(lhs_s[idx], s.astype(_BF16), preferred_element_type=_F32)
            outs[hd].append(r[HEAD_DIM:] + o2_s[idx])
            s_all[hd] = s * gam_s[idx][0:1, :] - r[:HEAD_DIM] + n_s[idx]
        yield
    for hd in heads:
        lo = hd * HEAD_DIM
        state[hd] = s_all[hd]
        z = z_ref[:, lo:lo + HEAD_DIM]
        o = _rmsnorm(jnp.concatenate(outs[hd], axis=0), gnw_ref[...]) * (z * _sigmoid(z))
        gdn_ref[:, lo:lo + HEAD_DIM] = o.astype(gdn_ref.dtype)
        yield


def _project_stage(x_ref, nw_ref, wconv_ref, wgate_ref, wba_ref, pbuf, halo, out):
    ts = x_ref.shape[0]
    hb = _rmsnorm(x_ref[...], nw_ref[...]).astype(_BF16)
    yield
    pbuf[0:CONV_HALO, :] = halo
    for c0 in range(0, wconv_ref.shape[1], PROJ_COLS):
        pbuf[CONV_HALO:CONV_HALO + ts, c0:c0 + PROJ_COLS] = jnp.dot(
            hb, wconv_ref[:, c0:c0 + PROJ_COLS], preferred_element_type=_F32)
        yield
    parts = []
    for c0 in range(0, wgate_ref.shape[1], PROJ_COLS):
        parts.append(jnp.dot(hb, wgate_ref[:, c0:c0 + PROJ_COLS], preferred_element_type=_F32))
        yield
    out["pg"] = jnp.concatenate(parts, axis=1)
    out["pba"] = jnp.dot(hb, wba_ref[...], preferred_element_type=_F32)


def _mixer_kernel(x_ref, nw_ref, wconv_ref, wgate_ref, wba_ref, cw_ref, lcb_ref,
                  alog_ref, dtb_ref, gnw_ref,
                  gdn_ref, xr_ref, gate_ref,
                  pbuf, pg_s, pba_s, z_s, lhs_s, n_s, o2_s, gam_s, state, *, tiles_per_seq,
                  n_tiles):
    s = pl.program_id(0)
    ts = x_ref.shape[0]
    gdn_width = gdn_ref.shape[1]

    @pl.when(s == 0)
    def _():
        for ref in (pbuf, pg_s, pba_s, z_s, lhs_s, n_s, o2_s, gam_s, state):
            ref[...] = jnp.zeros_like(ref)

    halo = jnp.where(s % tiles_per_seq == 0, 0.0, pbuf[ts:ts + CONV_HALO, :])
    proj, conv = {}, {}
    _interleave(
        (_gdn_operand_stage(pbuf, cw_ref, pba_s[...], alog_ref, dtb_ref, gdn_width,
                            lhs_s, n_s, o2_s, gam_s, conv), 4),
        (_project_stage(x_ref, nw_ref, wconv_ref, wgate_ref, wba_ref, pbuf, halo, proj), 1),
        (_gdn_recurrence_stage(lhs_s, n_s, o2_s, gam_s, z_s, gnw_ref, state, gdn_ref,
                               s % tiles_per_seq == 2 % tiles_per_seq), 1))

    pg_prev = pg_s[...]
    z_s[...] = pg_prev[:, :gdn_width]
    pg_s[...] = proj["pg"]
    pba_s[...] = proj["pba"]

    @pl.when((s >= 1) & (s <= n_tiles))
    def _():
        xr_ref[...] = (conv["lru_x"] + lcb_ref[...]).astype(xr_ref.dtype)
        gate_ref[...] = pg_prev[:, gdn_width:].astype(gate_ref.dtype)


def _mixer(x, nw, wconv, wgate, wba, cw, lcb, alog, dtb, gnw, *, ts, seq):
    n_tok, d_model = x.shape
    lru_width = lcb.shape[1]
    gdn_width = wgate.shape[1] - lru_width
    n_heads = gdn_width // HEAD_DIM
    n_tiles = n_tok // ts
    n_ops = n_heads * (ts // GDN_CHUNK)
    consts = (nw, wconv, wgate, wba, cw, lcb, alog, dtb, gnw)
    tile = lambda w, k: pl.BlockSpec(
        (ts, w), lambda s: (jnp.clip(s - k, 0, n_tiles - 1), 0))
    return pl.pallas_call(
        functools.partial(_mixer_kernel, tiles_per_seq=seq // ts, n_tiles=n_tiles),
        out_shape=(jax.ShapeDtypeStruct((n_tok, gdn_width), _BF16),
                   jax.ShapeDtypeStruct((n_tok, lru_width), _BF16),
                   jax.ShapeDtypeStruct((n_tok, lru_width), _BF16)),
        grid=(n_tiles + 2,),
        in_specs=[tile(d_model, 0)] + [_const_spec(c.shape) for c in consts],
        out_specs=(tile(gdn_width, 2), tile(lru_width, 1), tile(lru_width, 1)),
        scratch_shapes=[pltpu.VMEM((ts + CONV_HALO, wconv.shape[1]), _F32),
                        pltpu.VMEM((ts, wgate.shape[1]), _F32),
                        pltpu.VMEM((ts, LANES), _F32),
                        pltpu.VMEM((ts, gdn_width), _F32),
                        pltpu.VMEM((n_ops, HEAD_DIM + GDN_CHUNK, HEAD_DIM), _BF16),
                        pltpu.VMEM((n_ops, HEAD_DIM, HEAD_DIM), _F32),
                        pltpu.VMEM((n_ops, GDN_CHUNK, HEAD_DIM), _F32),
                        pltpu.VMEM((n_ops, SUBLANES, LANES), _F32),
                        pltpu.VMEM((n_heads, HEAD_DIM, HEAD_DIM), _F32)],
        compiler_params=pltpu.CompilerParams(
            dimension_semantics=("arbitrary",),
            vmem_limit_bytes=VMEM_LIMIT_BYTES),
        name="mixer",
    )(x, *consts)


def _mlp_stage(x_ref, gdn_ref, lru_s, wog_ref, wol_ref, nw_ref, w1_ref, w2_ref, fnw_ref, o_ref,
               ff_chunk):
    d_ff = w1_ref.shape[1]
    for r0 in range(0, x_ref.shape[0], MLP_ROWS):
        rows = slice(r0, r0 + MLP_ROWS)
        x1 = (x_ref[rows, :]
              + jnp.dot(gdn_ref[rows, :], wog_ref[...], preferred_element_type=_F32)
              + jnp.dot(lru_s[rows, :], wol_ref[...], preferred_element_type=_F32))
        yield
        mb = _rmsnorm(x1, nw_ref[...]).astype(_BF16)
        ff = None
        for c in range(d_ff // ff_chunk):
            c0 = c * ff_chunk
            u = jnp.maximum(
                jnp.dot(mb, w1_ref[:, c0:c0 + ff_chunk], preferred_element_type=_F32), 0.0)
            yield
            t = jnp.dot((u * u).astype(_BF16), w2_ref[c0:c0 + ff_chunk, :],
                        preferred_element_type=_F32)
            ff = t if ff is None else ff + t
            yield
        o_ref[rows, :] = _rmsnorm(x1 + ff, fnw_ref[...])


def _mlp_kernel(x_ref, gdn_ref, xr_ref, gate_ref, wog_ref, wol_ref, nw_ref, w1_ref, w2_ref,
                fnw_ref, wlru_ref, blru_ref, ap_ref, o_ref, lru_s, lru_next, hcarry, *, ff_chunk,
                tiles_per_seq):
    i = pl.program_id(0)

    @pl.when(i == 0)
    def _():
        lru_s[...] = jnp.zeros_like(lru_s)

    @pl.when(i % tiles_per_seq == 0)
    def _():
        hcarry[...] = jnp.zeros_like(hcarry)

    _interleave(
        (_mlp_stage(x_ref, gdn_ref, lru_s, wog_ref, wol_ref, nw_ref, w1_ref, w2_ref, fnw_ref,
                    o_ref, ff_chunk), 1),
        (_lru_stage(xr_ref, gate_ref, wlru_ref, blru_ref, ap_ref, hcarry, lru_next), 1))
    lru_s[...] = lru_next[...]


def _mlp(x, gdn, xr, gate, wog, wol, nw, w1, w2, fnw, wlru, blru, ap, *, tm, ff_chunk, seq):
    n_tok, d_model = x.shape
    n_tiles = n_tok // tm
    lru_width = xr.shape[1]
    front_tile = lambda w: pl.BlockSpec((tm, w), lambda i: (jnp.minimum(i, n_tiles - 1), 0))
    back_tile = lambda w: pl.BlockSpec((tm, w), lambda i: (jnp.maximum(i - 1, 0), 0))
    consts = (wog, wol, nw, w1, w2, fnw, wlru, blru, ap)
    return pl.pallas_call(
        functools.partial(_mlp_kernel, ff_chunk=ff_chunk, tiles_per_seq=seq // tm),
        out_shape=jax.ShapeDtypeStruct((n_tok, d_model), _F32),
        grid=(n_tiles + 1,),
        in_specs=[back_tile(d_model), back_tile(gdn.shape[1]), front_tile(lru_width),
                  front_tile(lru_width)] + [_const_spec(c.shape) for c in consts],
        out_specs=back_tile(d_model),
        scratch_shapes=[pltpu.VMEM((tm, lru_width), _BF16),
                        pltpu.VMEM((tm, lru_width), _BF16),
                        pltpu.VMEM((1, lru_width), _F32)],
        compiler_params=pltpu.CompilerParams(
            dimension_semantics=("arbitrary",),
            vmem_limit_bytes=VMEM_LIMIT_BYTES),
        name="outproj_mlp",
    )(x, gdn, xr, gate, *consts)


def _block_diag(w):
    g, i, j = w.shape
    eye = jnp.eye(g, dtype=w.dtype)
    return (eye[:, None, :, None] * w[:, :, None, :]).reshape(g * i, g * j)


def _layer(x, norm_mix_w, w_in, gdn_conv_w, gdn_A_log, gdn_dt_bias, gdn_norm_w,
           lru_conv_w, lru_conv_b, lru_gate_a_w, lru_gate_a_b, lru_gate_x_w, lru_gate_x_b,
           lru_a_param, w_out, norm_mlp_w, w_ff1, w_ff2, final_norm_w, *, seq_tile, tok_tile,
           ff_chunk):
    bsz, seq, d_model = x.shape
    n_heads = gdn_A_log.shape[0]
    gdn_width = n_heads * HEAD_DIM
    lru_width = lru_conv_b.shape[0]
    assert 2 * n_heads <= LANES
    assert seq % seq_tile == 0 and seq_tile % GDN_CHUNK == 0
    assert seq % tok_tile == 0 and w_ff1.shape[1] % ff_chunk == 0

    o_qkv, o_z = 0, 3 * gdn_width
    o_b = o_z + gdn_width
    o_a = o_b + n_heads
    o_lx = o_a + n_heads
    o_lg = o_lx + lru_width
    wconv = jnp.concatenate([w_in[:, o_qkv:o_z], w_in[:, o_lx:o_lg]], axis=1).astype(_BF16)
    wgate = jnp.concatenate([w_in[:, o_z:o_b], w_in[:, o_lg:o_lg + lru_width]], axis=1).astype(_BF16)
    wba = jnp.pad(w_in[:, o_b:o_lx], ((0, 0), (0, LANES - 2 * n_heads))).astype(_BF16)
    cw = jnp.concatenate([gdn_conv_w, lru_conv_w], axis=1).astype(_F32)
    pad_heads = lambda p: jnp.pad(p.astype(_F32), (n_heads, LANES - 2 * n_heads)).reshape(1, LANES)
    wlru = jnp.concatenate([_block_diag(lru_gate_a_w), _block_diag(lru_gate_x_w)], axis=1).astype(_BF16)
    blru = jnp.concatenate([lru_gate_a_b.reshape(1, -1), lru_gate_x_b.reshape(1, -1)], axis=1)

    n_tok = bsz * seq
    x2d = x.reshape(n_tok, d_model)
    gdn, xr, gate = _mixer(
        x2d, norm_mix_w.reshape(1, -1), wconv, wgate, wba, cw, lru_conv_b.reshape(1, -1),
        pad_heads(gdn_A_log), pad_heads(gdn_dt_bias), gdn_norm_w.reshape(1, -1),
        ts=seq_tile, seq=seq)

    out = _mlp(x2d, gdn, xr, gate,
               w_out[:gdn_width].astype(_BF16), w_out[gdn_width:].astype(_BF16),
               norm_mlp_w.reshape(1, -1), w_ff1.astype(_BF16), w_ff2.astype(_BF16),
               final_norm_w.reshape(1, -1), wlru, blru, lru_a_param.reshape(1, -1),
               tm=tok_tile, ff_chunk=ff_chunk, seq=seq)
    return out.reshape(bsz, seq, d_model)


def kernel(x, norm_mix_w, w_in, gdn_conv_w, gdn_A_log, gdn_dt_bias, gdn_norm_w, lru_conv_w, lru_conv_b, lru_gate_a_w, lru_gate_a_b, lru_gate_x_w, lru_gate_x_b, lru_a_param, w_out, norm_mlp_w, w_ff1, w_ff2, final_norm_w):
    assert norm_mix_w.shape[0] == 1, "single-layer stack"
    return _layer(x, norm_mix_w[0], w_in[0], gdn_conv_w[0], gdn_A_log[0], gdn_dt_bias[0],
                  gdn_norm_w[0], lru_conv_w[0], lru_conv_b[0], lru_gate_a_w[0], lru_gate_a_b[0],
                  lru_gate_x_w[0], lru_gate_x_b[0], lru_a_param[0], w_out[0], norm_mlp_w[0],
                  w_ff1[0], w_ff2[0], final_norm_w, seq_tile=256, tok_tile=512, ff_chunk=512)
```

```python
import functools
import math

import jax
import jax.numpy as jnp
from jax import lax
from jax.experimental import pallas as pl
from jax.experimental.pallas import tpu as pltpu

EPS = 1e-6
HEAD_DIM = 128
LANES = 128
SUBLANES = 8
GDN_CHUNK = 64
INV_BASE = 16
LRU_C = 8.0
CONV_WIDTH = 4
CONV_HALO = 8
PROJ_COLS = 256
LRU_ROWS = 128
F32_TINY = 1.1754944e-38
MXU_TILE = 256
VMEM_LIMIT_BYTES = 56 * 1024 * 1024

_BF16 = jnp.bfloat16
_F32 = jnp.float32
_NT_DIMS = (((1,), (1,)), ((), ()))


def _dot(a, b):
    return jnp.dot(a.astype(_BF16), b.astype(_BF16), preferred_element_type=_F32)


def _dot_nt(a, b):
    return lax.dot_general(a.astype(_BF16), b.astype(_BF16), _NT_DIMS,
                           preferred_element_type=_F32)


def _sigmoid(x):
    return 1.0 / (1.0 + jnp.exp(-x))


def _silu(x):
    hx = 0.5 * x
    return hx * jnp.tanh(hx) + hx


def _softplus(x):
    return jnp.maximum(x, 0.0) + jnp.log1p(jnp.exp(-jnp.abs(x)))


def _gelu_tanh(x):
    c = math.sqrt(2.0 / math.pi)
    return 0.5 * x * (1.0 + jnp.tanh(c * (x + 0.044715 * (x * x * x))))


def _rmsnorm(x, w):
    return x * lax.rsqrt(jnp.mean(x * x, axis=-1, keepdims=True) + EPS) * w


def _const_spec(shape):
    nd = len(shape)
    return pl.BlockSpec(shape, lambda *_: (0,) * nd, pipeline_mode=pl.Buffered(1))


def _interleave(*stages):
    live = list(stages)
    while live:
        for entry in list(live):
            stage, pieces = entry
            for _ in range(pieces):
                try:
                    next(stage)
                except StopIteration:
                    live.remove(entry)
                    break


def _causal_conv(pbuf, cw_ref, ts):
    pe = pbuf[...]
    pd = pltpu.roll(pe, 1, 0)
    near = cw_ref[3:4, :] * pe + cw_ref[2:3, :] * pd
    far = cw_ref[1:2, :] * pe + cw_ref[0:1, :] * pd
    y = near + pltpu.roll(far, 2, 0)
    return y[CONV_HALO:CONV_HALO + ts, :]


def _pack_chunks(m, lane_chunk):
    n = m.shape[0] // GDN_CHUNK
    out = m[(n - 1) * GDN_CHUNK:, :]
    for c in range(n - 2, -1, -1):
        out = jnp.where(lane_chunk == c, m[c * GDN_CHUNK:(c + 1) * GDN_CHUNK, :], out)
    return out


def _block_diag_of(p, bd_mask):
    pb = p.astype(_BF16)
    return jnp.concatenate([pb] * (bd_mask.shape[0] // GDN_CHUNK), axis=0) * bd_mask


def _packed_dot(x, p_bd):
    return jnp.dot(x.astype(_BF16), p_bd, preferred_element_type=_F32)


def _unit_lower_inverse(a_heads, r, j, bd_mask):
    c = GDN_CHUNK
    eye = (r == j).astype(_F32)
    base = (r // INV_BASE) == (j // INV_BASE)
    d = [jnp.where(base, a, 0.0) for a in a_heads]
    x = [eye - dh for dh in d]
    p = [_packed_dot(dh, _block_diag_of(dh, bd_mask)) for dh in d]
    span = 4
    while span < INV_BASE:
        both = [_packed_dot(jnp.concatenate([xh, ph], axis=0), _block_diag_of(ph, bd_mask))
                for xh, ph in zip(x, p)]
        x = [xh + bh[:c] for xh, bh in zip(x, both)]
        p = [bh[c:] for bh in both]
        span *= 2
    x = [xh + _packed_dot(xh, _block_diag_of(ph, bd_mask)) for xh, ph in zip(x, p)]
    size = INV_BASE
    while size < GDN_CHUNK:
        off = ((r // (2 * size)) == (j // (2 * size))) & ((r // size) != (j // size))
        xe = [_packed_dot(xh, _block_diag_of(jnp.where(off, a, 0.0), bd_mask))
              for xh, a in zip(x, a_heads)]
        x = [xh - _packed_dot(xeh, _block_diag_of(xh, bd_mask)) for xh, xeh in zip(x, xe)]
        size *= 2
    return x


def _gdn_chunk_operands(q_all, k_all, v_all, gb, lhs_s, n_s, o2_s, gam_s):
    n_heads = len(q_all)
    heads = range(n_heads)
    ts = gb.shape[0]
    n_chunks = ts // GDN_CHUNK
    row = lax.broadcasted_iota(jnp.int32, (ts, ts), 0)
    col = lax.broadcasted_iota(jnp.int32, (ts, ts), 1)
    same_chunk = (row // GDN_CHUNK) == (col // GDN_CHUNK)
    bd_mask = jnp.where(same_chunk, 1.0, 0.0).astype(_BF16)
    r = lax.broadcasted_iota(jnp.int32, (GDN_CHUNK, ts), 0)
    lane = lax.broadcasted_iota(jnp.int32, (GDN_CHUNK, ts), 1)
    lane_chunk = lane // GDN_CHUNK
    j = lane % GDN_CHUNK
    causal = r >= j
    strict = r > j

    ones_tri = jnp.where(same_chunk & (row >= col), 1.0, 0.0).astype(_BF16)
    gb_hi = gb.astype(_BF16)
    gb_r1 = gb - gb_hi.astype(_F32)
    gb_mid = gb_r1.astype(_BF16)
    gb_lo = (gb_r1 - gb_mid.astype(_F32)).astype(_BF16)
    gcum = (jnp.dot(ones_tri, gb_hi, preferred_element_type=_F32)
            + jnp.dot(ones_tri, gb_mid, preferred_element_type=_F32)
            + jnp.dot(ones_tri, gb_lo, preferred_element_type=_F32))
    gcum_t = gcum.T

    gcol_all, a_all, rhs_all, qk_all, qg_all = [], [], [], [], []
    for hd in heads:
        beta = gb[:, hd:hd + 1]
        gcol = gcum[:, n_heads + hd:n_heads + hd + 1]
        grow = gcum_t[n_heads + hd:n_heads + hd + 1, :]
        gpk = _pack_chunks(jnp.broadcast_to(gcol, (ts, ts)), lane_chunk)
        decay = jnp.where(causal, jnp.exp(jnp.where(causal, gpk - grow, 0.0)), 0.0)
        q, k, v = q_all[hd], k_all[hd], v_all[hd]
        egc = jnp.exp(gcol)
        kb = k * beta
        gcol_all.append(gcol)
        a_all.append(jnp.where(strict, _pack_chunks(_dot_nt(kb, k), lane_chunk) * decay, 0.0))
        rhs_all.append(jnp.concatenate([v * beta, kb * egc], axis=1))
        qk_all.append(_pack_chunks(_dot_nt(q, k), lane_chunk) * decay)
        qg_all.append(q * egc)
    t_all = _unit_lower_inverse(a_all, r, j, bd_mask)
    sol_all = [jnp.dot(_block_diag_of(t, bd_mask), rhs.astype(_BF16), preferred_element_type=_F32)
               for t, rhs in zip(t_all, rhs_all)]

    for c in range(n_chunks):
        r0 = c * GDN_CHUNK
        r1 = r0 + GDN_CHUNK
        for hd in heads:
            idx = hd * n_chunks + c
            sol = sol_all[hd][r0:r1]
            g_last = gcol_all[hd][r1 - 1:r1, :]
            k_dec = k_all[hd][r0:r1] * jnp.exp(g_last - gcol_all[hd][r0:r1])
            kd_sol = _dot(k_dec.T, sol)
            qk_sol = _dot(qk_all[hd][:, r0:r1], sol)
            lhs_s[idx] = jnp.concatenate(
                [kd_sol[:, HEAD_DIM:], qg_all[hd][r0:r1] - qk_sol[:, HEAD_DIM:]],
                axis=0).astype(lhs_s.dtype)
            n_s[idx] = kd_sol[:, :HEAD_DIM]
            o2_s[idx] = qk_sol[:, :HEAD_DIM]
            gam_s[idx] = jnp.broadcast_to(jnp.exp(g_last), gam_s.shape[1:])


def _gdn_recurrence_stage(lhs_s, n_s, o2_s, gam_s, z_ref, gnw_ref, state, gdn_ref):
    n_heads = state.shape[0]
    heads = range(n_heads)
    n_chunks = lhs_s.shape[0] // n_heads
    s_all = [state[hd] for hd in heads]
    outs = [[] for _ in heads]
    for c in range(n_chunks):
        for hd in heads:
            idx = hd * n_chunks + c
            s = s_all[hd]
            r = jnp.dot(lhs_s[idx], s.astype(_BF16), preferred_element_type=_F32)
            outs[hd].append(r[HEAD_DIM:] + o2_s[idx])
            s_all[hd] = s * gam_s[idx][0:1, :] - r[:HEAD_DIM] + n_s[idx]
        yield
    for hd in heads:
        lo = hd * HEAD_DIM
        state[hd] = s_all[hd]
        z = z_ref[:, lo:lo + HEAD_DIM]
        o = _rmsnorm(jnp.concatenate(outs[hd], axis=0), gnw_ref[...]) * _silu(z)
        gdn_ref[:, lo:lo + HEAD_DIM] = o.astype(gdn_ref.dtype)
        yield


def _project_stage(x_ref, nw_ref, wconv_ref, wgate_ref, wba_ref, pbuf, out):
    ts = x_ref.shape[0]
    hb = _rmsnorm(x_ref[...], nw_ref[...]).astype(_BF16)
    yield
    for c0 in range(0, wconv_ref.shape[1], PROJ_COLS):
        pbuf[CONV_HALO:CONV_HALO + ts, c0:c0 + PROJ_COLS] = jnp.dot(
            hb, wconv_ref[:, c0:c0 + PROJ_COLS], preferred_element_type=_F32)
        yield
    parts = []
    for c0 in range(0, wgate_ref.shape[1], PROJ_COLS):
        parts.append(jnp.dot(hb, wgate_ref[:, c0:c0 + PROJ_COLS], preferred_element_type=_F32))
        yield
    out["pg"] = jnp.concatenate(parts, axis=1)
    out["pba"] = jnp.dot(hb, wba_ref[...], preferred_element_type=_F32)


def _mixer_kernel(x_ref, nw_ref, wconv_ref, wgate_ref, wba_ref, cw_ref, lcb_ref,
                  alog_ref, dtb_ref, gnw_ref,
                  gdn_ref, xr_ref, gate_ref,
                  pbuf, z_s, lhs_s, n_s, o2_s, gam_s, state, *, tiles_per_seq, n_tiles):
    j = pl.program_id(0)
    ts = x_ref.shape[0]
    gdn_width = gdn_ref.shape[1]
    n_heads = gdn_width // HEAD_DIM

    @pl.when(j == 0)
    def _():
        for ref in (z_s, lhs_s, n_s, o2_s, gam_s):
            ref[...] = jnp.zeros_like(ref)

    @pl.when(j % tiles_per_seq == 0)
    def _():
        pbuf[0:CONV_HALO, :] = jnp.zeros((CONV_HALO, pbuf.shape[1]), _F32)

    @pl.when((j == 0) | (j % tiles_per_seq == 1 % tiles_per_seq))
    def _():
        state[...] = jnp.zeros_like(state)

    proj = {}
    _interleave(
        (_project_stage(x_ref, nw_ref, wconv_ref, wgate_ref, wba_ref, pbuf, proj), 1),
        (_gdn_recurrence_stage(lhs_s, n_s, o2_s, gam_s, z_s, gnw_ref, state, gdn_ref), 1))

    y = _causal_conv(pbuf, cw_ref, ts)
    pbuf[0:CONV_HALO, :] = pbuf[ts:ts + CONV_HALO, :]

    pba = proj["pba"]
    lane = lax.broadcasted_iota(jnp.int32, pba.shape, 1)
    gb = jnp.where(lane < n_heads, _sigmoid(pba),
                   -jnp.exp(alog_ref[...]) * _softplus(pba + dtb_ref[...]))

    qkv = y[:, :3 * gdn_width]
    qkv = _silu(qkv)
    scale = HEAD_DIM ** -0.5
    q_all, k_all, v_all = [], [], []
    for hd in range(n_heads):
        lo = hd * HEAD_DIM
        qh = qkv[:, lo:lo + HEAD_DIM]
        kh = qkv[:, gdn_width + lo:gdn_width + lo + HEAD_DIM]
        q_all.append(qh * (lax.rsqrt(jnp.sum(qh * qh, axis=-1, keepdims=True) + EPS) * scale))
        k_all.append(kh * lax.rsqrt(jnp.sum(kh * kh, axis=-1, keepdims=True) + EPS))
        v_all.append(qkv[:, 2 * gdn_width + lo:2 * gdn_width + lo + HEAD_DIM])

    _gdn_chunk_operands(q_all, k_all, v_all, gb, lhs_s, n_s, o2_s, gam_s)
    pg = proj["pg"]
    z_s[...] = pg[:, :gdn_width]

    @pl.when(j < n_tiles)
    def _():
        xr_ref[...] = (y[:, 3 * gdn_width:] + lcb_ref[...]).astype(xr_ref.dtype)
        gate_ref[...] = pg[:, gdn_width:].astype(gate_ref.dtype)


def _mixer(x, nw, wconv, wgate, wba, cw, lcb, alog, dtb, gnw, *, ts, seq):
    n_tok, d_model = x.shape
    lru_width = lcb.shape[1]
    gdn_width = wgate.shape[1] - lru_width
    n_heads = gdn_width // HEAD_DIM
    n_tiles = n_tok // ts
    n_ops = n_heads * (ts // GDN_CHUNK)
    consts = (nw, wconv, wgate, wba, cw, lcb, alog, dtb, gnw)
    front_tile = lambda w: pl.BlockSpec((ts, w), lambda j: (jnp.minimum(j, n_tiles - 1), 0))
    back_tile = lambda w: pl.BlockSpec((ts, w), lambda j: (jnp.maximum(j - 1, 0), 0))
    return pl.pallas_call(
        functools.partial(_mixer_kernel, tiles_per_seq=seq // ts, n_tiles=n_tiles),
        out_shape=(jax.ShapeDtypeStruct((n_tok, gdn_width), _BF16),
                   jax.ShapeDtypeStruct((n_tok, lru_width), _BF16),
                   jax.ShapeDtypeStruct((n_tok, lru_width), _BF16)),
        grid=(n_tiles + 1,),
        in_specs=[front_tile(d_model)] + [_const_spec(c.shape) for c in consts],
        out_specs=(back_tile(gdn_width), front_tile(lru_width), front_tile(lru_width)),
        scratch_shapes=[pltpu.VMEM((ts + CONV_HALO, wconv.shape[1]), _F32),
                        pltpu.VMEM((ts, gdn_width), _F32),
                        pltpu.VMEM((n_ops, HEAD_DIM + GDN_CHUNK, HEAD_DIM), _BF16),
                        pltpu.VMEM((n_ops, HEAD_DIM, HEAD_DIM), _F32),
                        pltpu.VMEM((n_ops, GDN_CHUNK, HEAD_DIM), _F32),
                        pltpu.VMEM((n_ops, SUBLANES, LANES), _F32),
                        pltpu.VMEM((n_heads, HEAD_DIM, HEAD_DIM), _F32)],
        compiler_params=pltpu.CompilerParams(
            dimension_semantics=("arbitrary",),
            vmem_limit_bytes=VMEM_LIMIT_BYTES),
        name="mixer",
    )(x, *consts)


def _block_diag_dot(xb, w_ref, col0):
    k = xb.shape[1]
    return jnp.concatenate(
        [jnp.dot(xb[:, t:t + MXU_TILE], w_ref[t:t + MXU_TILE, col0 + t:col0 + t + MXU_TILE],
                 preferred_element_type=_F32) for t in range(0, k, MXU_TILE)], axis=1)


def _lru_stage(xr_ref, gate_ref, wlru_ref, blru_ref, ap_ref, hcarry, lru_ref):
    ts, lru_width = xr_ref.shape
    nsp = _softplus(-ap_ref[...])
    row = lax.broadcasted_iota(jnp.int32, (LRU_ROWS, lru_width), 0) % SUBLANES
    carry = hcarry[...]
    for b0 in range(0, ts, LRU_ROWS):
        xb = xr_ref[b0:b0 + LRU_ROWS, :]
        xr = xb.astype(_F32)
        r = _sigmoid(_block_diag_dot(xb, wlru_ref, 0) + blru_ref[:, :lru_width])
        log_a = -LRU_C * r * nsp
        a = jnp.exp(log_a)
        yield
        i = _sigmoid(_block_diag_dot(xb, wlru_ref, lru_width) + blru_ref[:, lru_width:])
        v = jnp.maximum(1.0 - a * a, 0.0)
        b = v * lax.rsqrt(jnp.maximum(v, F32_TINY)) * (i * xr)
        yield
        d = 1
        while d < SUBLANES:
            keep = row >= d
            a_sh = jnp.where(keep, pltpu.roll(a, d, 0), 1.0)
            b_sh = jnp.where(keep, pltpu.roll(b, d, 0), 0.0)
            b = a * b_sh + b
            a = a * a_sh
            d *= 2
        yield
        groups = []
        for r0 in range(0, LRU_ROWS, SUBLANES):
            hg = b[r0:r0 + SUBLANES] + a[r0:r0 + SUBLANES] * carry
            carry = hg[SUBLANES - 1:SUBLANES]
            groups.append(hg)
        gate = gate_ref[b0:b0 + LRU_ROWS, :].astype(_F32)
        lru_ref[b0:b0 + LRU_ROWS, :] = (jnp.concatenate(groups, axis=0)
                                        * _gelu_tanh(gate)).astype(lru_ref.dtype)
        yield
    hcarry[...] = carry


def _mlp_stage(x_ref, gdn_ref, lru_s, wog_ref, wol_ref, nw_ref, w1_ref, w2_ref, fnw_ref, o_ref,
               ff_chunk):
    d_ff = w1_ref.shape[1]
    x1 = (x_ref[...]
          + jnp.dot(gdn_ref[...], wog_ref[...], preferred_element_type=_F32)
          + jnp.dot(lru_s[...], wol_ref[...], preferred_element_type=_F32))
    yield
    mb = _rmsnorm(x1, nw_ref[...]).astype(_BF16)
    ff = None
    for c0 in range(0, d_ff, ff_chunk):
        u = jnp.maximum(
            jnp.dot(mb, w1_ref[:, c0:c0 + ff_chunk], preferred_element_type=_F32), 0.0)
        yield
        t = jnp.dot((u * u).astype(_BF16), w2_ref[c0:c0 + ff_chunk, :],
                    preferred_element_type=_F32)
        ff = t if ff is None else ff + t
        yield
    o_ref[...] = _rmsnorm(x1 + ff, fnw_ref[...])


def _mlp_kernel(x_ref, gdn_ref, xr_ref, gate_ref, wog_ref, wol_ref, nw_ref, w1_ref, w2_ref,
                fnw_ref, wlru_ref, blru_ref, ap_ref, o_ref, lru_s, lru_next, hcarry, *, ff_chunk,
                tiles_per_seq):
    i = pl.program_id(0)

    @pl.when(i == 0)
    def _():
        lru_s[...] = jnp.zeros_like(lru_s)

    @pl.when(i % tiles_per_seq == 0)
    def _():
        hcarry[...] = jnp.zeros_like(hcarry)

    _interleave(
        (_mlp_stage(x_ref, gdn_ref, lru_s, wog_ref, wol_ref, nw_ref, w1_ref, w2_ref, fnw_ref,
                    o_ref, ff_chunk), 1),
        (_lru_stage(xr_ref, gate_ref, wlru_ref, blru_ref, ap_ref, hcarry, lru_next), 1))
    lru_s[...] = lru_next[...]


def _mlp(x, gdn, xr, gate, wog, wol, nw, w1, w2, fnw, wlru, blru, ap, *, tm, ff_chunk, seq):
    n_tok, d_model = x.shape
    n_tiles = n_tok // tm
    lru_width = xr.shape[1]
    front_tile = lambda w: pl.BlockSpec((tm, w), lambda i: (jnp.minimum(i, n_tiles - 1), 0))
    back_tile = lambda w: pl.BlockSpec((tm, w), lambda i: (jnp.maximum(i - 1, 0), 0))
    consts = (wog, wol, nw, w1, w2, fnw, wlru, blru, ap)
    return pl.pallas_call(
        functools.partial(_mlp_kernel, ff_chunk=ff_chunk, tiles_per_seq=seq // tm),
        out_shape=jax.ShapeDtypeStruct((n_tok, d_model), _F32),
        grid=(n_tiles + 1,),
        in_specs=[back_tile(d_model), back_tile(gdn.shape[1]), front_tile(lru_width),
                  front_tile(lru_width)] + [_const_spec(c.shape) for c in consts],
        out_specs=back_tile(d_model),
        scratch_shapes=[pltpu.VMEM((tm, lru_width), _BF16),
                        pltpu.VMEM((tm, lru_width), _BF16),
                        pltpu.VMEM((1, lru_width), _F32)],
        compiler_params=pltpu.CompilerParams(
            dimension_semantics=("arbitrary",),
            vmem_limit_bytes=VMEM_LIMIT_BYTES),
        name="outproj_mlp",
    )(x, gdn, xr, gate, *consts)


def _block_diag(w):
    g, i, j = w.shape
    eye = jnp.eye(g, dtype=w.dtype)
    return (eye[:, None, :, None] * w[:, :, None, :]).reshape(g * i, g * j)


def _layer(x, norm_mix_w, w_in, gdn_conv_w, gdn_A_log, gdn_dt_bias, gdn_norm_w,
           lru_conv_w, lru_conv_b, lru_gate_a_w, lru_gate_a_b, lru_gate_x_w, lru_gate_x_b,
           lru_a_param, w_out, norm_mlp_w, w_ff1, w_ff2, final_norm_w, *, seq_tile, tok_tile,
           ff_chunk):
    bsz, seq, d_model = x.shape
    n_heads = gdn_A_log.shape[0]
    gdn_width = n_heads * HEAD_DIM
    lru_width = lru_conv_b.shape[0]
    assert 2 * n_heads <= LANES
    assert seq % seq_tile == 0 and seq_tile % GDN_CHUNK == 0
    assert seq % tok_tile == 0 and tok_tile % LRU_ROWS == 0 and w_ff1.shape[1] % ff_chunk == 0
    assert MXU_TILE % lru_gate_a_w.shape[1] == 0 and lru_width % MXU_TILE == 0

    o_qkv, o_z = 0, 3 * gdn_width
    o_b = o_z + gdn_width
    o_a = o_b + n_heads
    o_lx = o_a + n_heads
    o_lg = o_lx + lru_width
    wconv = jnp.concatenate([w_in[:, o_qkv:o_z], w_in[:, o_lx:o_lg]], axis=1).astype(_BF16)
    wgate = jnp.concatenate([w_in[:, o_z:o_b], w_in[:, o_lg:o_lg + lru_width]], axis=1).astype(_BF16)
    wba = jnp.pad(w_in[:, o_b:o_lx], ((0, 0), (0, LANES - 2 * n_heads))).astype(_BF16)
    cw = jnp.concatenate([gdn_conv_w, lru_conv_w], axis=1).astype(_F32)
    pad_heads = lambda p: jnp.pad(p.astype(_F32), (n_heads, LANES - 2 * n_heads)).reshape(1, LANES)
    wlru = jnp.concatenate([_block_diag(lru_gate_a_w), _block_diag(lru_gate_x_w)], axis=1).astype(_BF16)
    blru = jnp.concatenate([lru_gate_a_b.reshape(1, -1), lru_gate_x_b.reshape(1, -1)], axis=1)

    n_tok = bsz * seq
    x2d = x.reshape(n_tok, d_model)
    gdn, xr, gate = _mixer(
        x2d, norm_mix_w.reshape(1, -1), wconv, wgate, wba, cw, lru_conv_b.reshape(1, -1),
        pad_heads(gdn_A_log), pad_heads(gdn_dt_bias), gdn_norm_w.reshape(1, -1),
        ts=seq_tile, seq=seq)

    out = _mlp(x2d, gdn, xr, gate,
               w_out[:gdn_width].astype(_BF16), w_out[gdn_width:].astype(_BF16),
               norm_mlp_w.reshape(1, -1), w_ff1.astype(_BF16), w_ff2.astype(_BF16),
               final_norm_w.reshape(1, -1), wlru, blru, lru_a_param.reshape(1, -1),
               tm=tok_tile, ff_chunk=ff_chunk, seq=seq)
    return out.reshape(bsz, seq, d_model)


def kernel(x, norm_mix_w, w_in, gdn_conv_w, gdn_A_log, gdn_dt_bias, gdn_norm_w, lru_conv_w, lru_conv_b, lru_gate_a_w, lru_gate_a_b, lru_gate_x_w, lru_gate_x_b, lru_a_param, w_out, norm_mlp_w, w_ff1, w_ff2, final_norm_w):
    assert norm_mix_w.shape[0] == 1, "single-layer stack"
    return _layer(x, norm_mix_w[0], w_in[0], gdn_conv_w[0], gdn_A_log[0], gdn_dt_bias[0],
                  gdn_norm_w[0], lru_conv_w[0], lru_conv_b[0], lru_gate_a_w[0], lru_gate_a_b[0],
                  lru_gate_x_w[0], lru_gate_x_b[0], lru_a_param[0], w_out[0], norm_mlp_w[0],
                  w_ff1[0], w_ff2[0], final_norm_w, seq_tile=256, tok_tile=512, ff_chunk=512)
```

```python
import functools
import math

import jax
import jax.numpy as jnp
from jax import lax
from jax.experimental import pallas as pl
from jax.experimental.pallas import tpu as pltpu

EPS = 1e-6
HEAD_DIM = 128
LANES = 128
SUBLANES = 8
GDN_CHUNK = 64
INV_BASE = 16
LRU_C = 8.0
CONV_WIDTH = 4
CONV_HALO = 8
PROJ_COLS = 256
LRU_ROWS = 256
F32_TINY = 1.1754944e-38
MXU_TILE = 256
VMEM_LIMIT_BYTES = 56 * 1024 * 1024

_BF16 = jnp.bfloat16
_F32 = jnp.float32
_NT_DIMS = (((1,), (1,)), ((), ()))


def _dot(a, b):
    return jnp.dot(a.astype(_BF16), b.astype(_BF16), preferred_element_type=_F32)


def _dot_nt(a, b):
    return lax.dot_general(a.astype(_BF16), b.astype(_BF16), _NT_DIMS,
                           preferred_element_type=_F32)


def _sigmoid(x):
    return 1.0 / (1.0 + jnp.exp(-x))


def _silu(x):
    hx = 0.5 * x
    return hx * jnp.tanh(hx) + hx


def _softplus(x):
    return jnp.maximum(x, 0.0) + jnp.log1p(jnp.exp(-jnp.abs(x)))


def _gelu_tanh(x):
    c = math.sqrt(2.0 / math.pi)
    hx = 0.5 * x
    return hx * jnp.tanh(x * (c + (c * 0.044715) * (x * x))) + hx


def _rmsnorm(x, w):
    return x * lax.rsqrt(jnp.mean(x * x, axis=-1, keepdims=True) + EPS) * w


def _const_spec(shape):
    nd = len(shape)
    return pl.BlockSpec(shape, lambda *_: (0,) * nd, pipeline_mode=pl.Buffered(1))


def _interleave(*stages):
    live = list(stages)
    while live:
        for entry in list(live):
            stage, pieces = entry
            for _ in range(pieces):
                try:
                    next(stage)
                except StopIteration:
                    live.remove(entry)
                    break


def _causal_conv(pbuf, cw_ref, ts):
    pe = pbuf[...]
    pd = pltpu.roll(pe, 1, 0)
    near = cw_ref[3:4, :] * pe + cw_ref[2:3, :] * pd
    far = cw_ref[1:2, :] * pe + cw_ref[0:1, :] * pd
    y = near + pltpu.roll(far, 2, 0)
    return y[CONV_HALO:CONV_HALO + ts, :]


def _pack_chunks(m, lane_chunk):
    n = m.shape[0] // GDN_CHUNK
    out = m[(n - 1) * GDN_CHUNK:, :]
    for c in range(n - 2, -1, -1):
        out = jnp.where(lane_chunk == c, m[c * GDN_CHUNK:(c + 1) * GDN_CHUNK, :], out)
    return out


def _block_diag_of(p, bd_mask):
    pb = p.astype(_BF16)
    return jnp.concatenate([pb] * (bd_mask.shape[0] // GDN_CHUNK), axis=0) * bd_mask


def _packed_dot(x, p_bd):
    return jnp.dot(x.astype(_BF16), p_bd, preferred_element_type=_F32)


def _unit_lower_inverse(a_heads, r, j, bd_mask):
    c = GDN_CHUNK
    eye = (r == j).astype(_F32)
    base = (r // INV_BASE) == (j // INV_BASE)
    d = [jnp.where(base, a, 0.0) for a in a_heads]
    x = [eye - dh for dh in d]
    p = [_packed_dot(dh, _block_diag_of(dh, bd_mask)) for dh in d]
    span = 4
    while span < INV_BASE:
        both = [_packed_dot(jnp.concatenate([xh, ph], axis=0), _block_diag_of(ph, bd_mask))
                for xh, ph in zip(x, p)]
        x = [xh + bh[:c] for xh, bh in zip(x, both)]
        p = [bh[c:] for bh in both]
        span *= 2
    x = [xh + _packed_dot(xh, _block_diag_of(ph, bd_mask)) for xh, ph in zip(x, p)]
    size = INV_BASE
    while size < GDN_CHUNK:
        off = ((r // (2 * size)) == (j // (2 * size))) & ((r // size) != (j // size))
        xe = [_packed_dot(xh, _block_diag_of(jnp.where(off, a, 0.0), bd_mask))
              for xh, a in zip(x, a_heads)]
        x = [xh - _packed_dot(xeh, _block_diag_of(xh, bd_mask)) for xh, xeh in zip(x, xe)]
        size *= 2
    return x


def _gdn_chunk_operands(q_all, k_all, v_all, gb, lhs_s, n_s, o2_s, gam_s):
    n_heads = len(q_all)
    heads = range(n_heads)
    ts = gb.shape[0]
    n_chunks = ts // GDN_CHUNK
    row = lax.broadcasted_iota(jnp.int32, (ts, ts), 0)
    col = lax.broadcasted_iota(jnp.int32, (ts, ts), 1)
    same_chunk = (row // GDN_CHUNK) == (col // GDN_CHUNK)
    bd_mask = jnp.where(same_chunk, 1.0, 0.0).astype(_BF16)
    r = lax.broadcasted_iota(jnp.int32, (GDN_CHUNK, ts), 0)
    lane = lax.broadcasted_iota(jnp.int32, (GDN_CHUNK, ts), 1)
    lane_chunk = lane // GDN_CHUNK
    j = lane % GDN_CHUNK
    causal = r >= j
    strict = r > j

    ones_tri = jnp.where(same_chunk & (row >= col), 1.0, 0.0).astype(_BF16)
    gb_hi = gb.astype(_BF16)
    gb_r1 = gb - gb_hi.astype(_F32)
    gb_mid = gb_r1.astype(_BF16)
    gb_lo = (gb_r1 - gb_mid.astype(_F32)).astype(_BF16)
    gcum = (jnp.dot(ones_tri, gb_hi, preferred_element_type=_F32)
            + jnp.dot(ones_tri, gb_mid, preferred_element_type=_F32)
            + jnp.dot(ones_tri, gb_lo, preferred_element_type=_F32))
    gcum_t = gcum.T

    gcol_all, a_all, rhs_all, qk_all, qg_all = [], [], [], [], []
    for hd in heads:
        beta = gb[:, hd:hd + 1]
        gcol = gcum[:, n_heads + hd:n_heads + hd + 1]
        grow = gcum_t[n_heads + hd:n_heads + hd + 1, :]
        gpk = _pack_chunks(jnp.broadcast_to(gcol, (ts, ts)), lane_chunk)
        decay = jnp.where(causal, jnp.exp(jnp.where(causal, gpk - grow, 0.0)), 0.0)
        q, k, v = q_all[hd], k_all[hd], v_all[hd]
        egc = jnp.exp(gcol)
        kb = k * beta
        gcol_all.append(gcol)
        a_all.append(jnp.where(strict, _pack_chunks(_dot_nt(kb, k), lane_chunk) * decay, 0.0))
        rhs_all.append(jnp.concatenate([v * beta, kb * egc], axis=1))
        qk_all.append(_pack_chunks(_dot_nt(q, k), lane_chunk) * decay)
        qg_all.append(q * egc)
    t_all = _unit_lower_inverse(a_all, r, j, bd_mask)
    sol_all = [jnp.dot(_block_diag_of(t, bd_mask), rhs.astype(_BF16), preferred_element_type=_F32)
               for t, rhs in zip(t_all, rhs_all)]

    for c in range(n_chunks):
        r0 = c * GDN_CHUNK
        r1 = r0 + GDN_CHUNK
        for hd in heads:
            idx = hd * n_chunks + c
            sol = sol_all[hd][r0:r1]
            g_last = gcol_all[hd][r1 - 1:r1, :]
            k_dec = k_all[hd][r0:r1] * jnp.exp(g_last - gcol_all[hd][r0:r1])
            kd_sol = _dot(k_dec.T, sol)
            qk_sol = _dot(qk_all[hd][:, r0:r1], sol)
            lhs_s[idx] = jnp.concatenate(
                [kd_sol[:, HEAD_DIM:], qg_all[hd][r0:r1] - qk_sol[:, HEAD_DIM:]],
                axis=0).astype(lhs_s.dtype)
            n_s[idx] = kd_sol[:, :HEAD_DIM]
            o2_s[idx] = qk_sol[:, :HEAD_DIM]
            gam_s[idx] = jnp.broadcast_to(jnp.exp(g_last), gam_s.shape[1:])


def _gdn_recurrence_stage(lhs_s, n_s, o2_s, gam_s, z_ref, gnw_ref, state, gdn_ref):
    n_heads = state.shape[0]
    heads = range(n_heads)
    n_chunks = lhs_s.shape[0] // n_heads
    s_all = [state[hd] for hd in heads]
    outs = [[] for _ in heads]
    for c in range(n_chunks):
        for hd in heads:
            idx = hd * n_chunks + c
            s = s_all[hd]
            r = jnp.dot(lhs_s[idx], s.astype(_BF16), preferred_element_type=_F32)
            outs[hd].append(r[HEAD_DIM:] + o2_s[idx])
            s_all[hd] = s * gam_s[idx][0:1, :] - r[:HEAD_DIM] + n_s[idx]
        yield
    for hd in heads:
        lo = hd * HEAD_DIM
        state[hd] = s_all[hd]
        z = z_ref[:, lo:lo + HEAD_DIM]
        o = _rmsnorm(jnp.concatenate(outs[hd], axis=0), gnw_ref[...]) * _silu(z)
        gdn_ref[:, lo:lo + HEAD_DIM] = o.astype(gdn_ref.dtype)
        yield


def _project_stage(x_ref, nw_ref, wconv_ref, wgate_ref, wba_ref, pbuf, out):
    ts = x_ref.shape[0]
    hb = _rmsnorm(x_ref[...], nw_ref[...]).astype(_BF16)
    yield
    for c0 in range(0, wconv_ref.shape[1], PROJ_COLS):
        pbuf[CONV_HALO:CONV_HALO + ts, c0:c0 + PROJ_COLS] = jnp.dot(
            hb, wconv_ref[:, c0:c0 + PROJ_COLS], preferred_element_type=_F32)
        yield
    parts = []
    for c0 in range(0, wgate_ref.shape[1], PROJ_COLS):
        parts.append(jnp.dot(hb, wgate_ref[:, c0:c0 + PROJ_COLS], preferred_element_type=_F32))
        yield
    out["pg"] = jnp.concatenate(parts, axis=1)
    out["pba"] = jnp.dot(hb, wba_ref[...], preferred_element_type=_F32)


def _mixer_kernel(x_ref, nw_ref, wconv_ref, wgate_ref, wba_ref, cw_ref, lcb_ref,
                  alog_ref, dtb_ref, gnw_ref,
                  gdn_ref, xr_ref, gate_ref,
                  pbuf, z_s, lhs_s, n_s, o2_s, gam_s, state, *, tiles_per_seq, n_tiles):
    j = pl.program_id(0)
    ts = x_ref.shape[0]
    gdn_width = gdn_ref.shape[1]
    n_heads = gdn_width // HEAD_DIM

    @pl.when(j == 0)
    def _():
        for ref in (z_s, lhs_s, n_s, o2_s, gam_s):
            ref[...] = jnp.zeros_like(ref)

    @pl.when(j % tiles_per_seq == 0)
    def _():
        pbuf[0:CONV_HALO, :] = jnp.zeros((CONV_HALO, pbuf.shape[1]), _F32)

    @pl.when((j == 0) | (j % tiles_per_seq == 1 % tiles_per_seq))
    def _():
        state[...] = jnp.zeros_like(state)

    proj = {}
    _interleave(
        (_project_stage(x_ref, nw_ref, wconv_ref, wgate_ref, wba_ref, pbuf, proj), 1),
        (_gdn_recurrence_stage(lhs_s, n_s, o2_s, gam_s, z_s, gnw_ref, state, gdn_ref), 1))

    y = _causal_conv(pbuf, cw_ref, ts)
    pbuf[0:CONV_HALO, :] = pbuf[ts:ts + CONV_HALO, :]

    pba = proj["pba"]
    lane = lax.broadcasted_iota(jnp.int32, pba.shape, 1)
    gb = jnp.where(lane < n_heads, _sigmoid(pba),
                   -jnp.exp(alog_ref[...]) * _softplus(pba + dtb_ref[...]))

    qkv = y[:, :3 * gdn_width]
    qkv = _silu(qkv)
    scale = HEAD_DIM ** -0.5
    q_all, k_all, v_all = [], [], []
    for hd in range(n_heads):
        lo = hd * HEAD_DIM
        qh = qkv[:, lo:lo + HEAD_DIM]
        kh = qkv[:, gdn_width + lo:gdn_width + lo + HEAD_DIM]
        q_all.append(qh * (lax.rsqrt(jnp.sum(qh * qh, axis=-1, keepdims=True) + EPS) * scale))
        k_all.append(kh * lax.rsqrt(jnp.sum(kh * kh, axis=-1, keepdims=True) + EPS))
        v_all.append(qkv[:, 2 * gdn_width + lo:2 * gdn_width + lo + HEAD_DIM])

    _gdn_chunk_operands(q_all, k_all, v_all, gb, lhs_s, n_s, o2_s, gam_s)
    pg = proj["pg"]
    z_s[...] = pg[:, :gdn_width]

    @pl.when(j < n_tiles)
    def _():
        xr_ref[...] = (y[:, 3 * gdn_width:] + lcb_ref[...]).astype(xr_ref.dtype)
        gate_ref[...] = pg[:, gdn_width:].astype(gate_ref.dtype)


def _mixer(x, nw, wconv, wgate, wba, cw, lcb, alog, dtb, gnw, *, ts, seq):
    n_tok, d_model = x.shape
    lru_width = lcb.shape[1]
    gdn_width = wgate.shape[1] - lru_width
    n_heads = gdn_width // HEAD_DIM
    n_tiles = n_tok // ts
    n_ops = n_heads * (ts // GDN_CHUNK)
    consts = (nw, wconv, wgate, wba, cw, lcb, alog, dtb, gnw)
    front_tile = lambda w: pl.BlockSpec((ts, w), lambda j: (jnp.minimum(j, n_tiles - 1), 0))
    back_tile = lambda w: pl.BlockSpec((ts, w), lambda j: (jnp.maximum(j - 1, 0), 0))
    return pl.pallas_call(
        functools.partial(_mixer_kernel, tiles_per_seq=seq // ts, n_tiles=n_tiles),
        out_shape=(jax.ShapeDtypeStruct((n_tok, gdn_width), _BF16),
                   jax.ShapeDtypeStruct((n_tok, lru_width), _BF16),
                   jax.ShapeDtypeStruct((n_tok, lru_width), _BF16)),
        grid=(n_tiles + 1,),
        in_specs=[front_tile(d_model)] + [_const_spec(c.shape) for c in consts],
        out_specs=(back_tile(gdn_width), front_tile(lru_width), front_tile(lru_width)),
        scratch_shapes=[pltpu.VMEM((ts + CONV_HALO, wconv.shape[1]), _F32),
                        pltpu.VMEM((ts, gdn_width), _F32),
                        pltpu.VMEM((n_ops, HEAD_DIM + GDN_CHUNK, HEAD_DIM), _BF16),
                        pltpu.VMEM((n_ops, HEAD_DIM, HEAD_DIM), _F32),
                        pltpu.VMEM((n_ops, GDN_CHUNK, HEAD_DIM), _F32),
                        pltpu.VMEM((n_ops, SUBLANES, LANES), _F32),
                        pltpu.VMEM((n_heads, HEAD_DIM, HEAD_DIM), _F32)],
        compiler_params=pltpu.CompilerParams(
            dimension_semantics=("arbitrary",),
            vmem_limit_bytes=VMEM_LIMIT_BYTES),
        name="mixer",
    )(x, *consts)


def _block_diag_dot(xb, w_ref, col0):
    k = xb.shape[1]
    return jnp.concatenate(
        [jnp.dot(xb[:, t:t + MXU_TILE], w_ref[t:t + MXU_TILE, col0 + t:col0 + t + MXU_TILE],
                 preferred_element_type=_F32) for t in range(0, k, MXU_TILE)], axis=1)


def _lru_stage(xr_ref, gate_ref, wlru_ref, blru_ref, ap_ref, hcarry, lru_ref):
    ts, lru_width = xr_ref.shape
    nsp = _softplus(-ap_ref[...])
    row = lax.broadcasted_iota(jnp.int32, (LRU_ROWS // SUBLANES, SUBLANES, lru_width), 1)
    carry = hcarry[...]
    for b0 in range(0, ts, LRU_ROWS):
        xb = xr_ref[b0:b0 + LRU_ROWS, :]
        xr = xb.astype(_F32)
        r = _sigmoid(_block_diag_dot(xb, wlru_ref, 0) + blru_ref[:, :lru_width])
        log_a = -LRU_C * r * nsp
        a = jnp.exp(log_a)
        yield
        i = _sigmoid(_block_diag_dot(xb, wlru_ref, lru_width) + blru_ref[:, lru_width:])
        v = jnp.maximum(1.0 - a * a, 0.0)
        b = v * lax.rsqrt(jnp.maximum(v, F32_TINY)) * (i * xr)
        yield
        a = a.reshape(LRU_ROWS // SUBLANES, SUBLANES, lru_width)
        b = b.reshape(LRU_ROWS // SUBLANES, SUBLANES, lru_width)
        d = 1
        while d < SUBLANES:
            keep = row >= d
            a_sh = jnp.where(keep, pltpu.roll(a, d, 1), 1.0)
            b_sh = jnp.where(keep, pltpu.roll(b, d, 1), 0.0)
            b = a * b_sh + b
            a = a * a_sh
            d *= 2
        a = a.reshape(LRU_ROWS, lru_width)
        b = b.reshape(LRU_ROWS, lru_width)
        yield
        groups = []
        for r0 in range(0, LRU_ROWS, SUBLANES):
            hg = b[r0:r0 + SUBLANES] + a[r0:r0 + SUBLANES] * carry
            carry = hg[SUBLANES - 1:SUBLANES]
            groups.append(hg)
        gate = gate_ref[b0:b0 + LRU_ROWS, :].astype(_F32)
        lru_ref[b0:b0 + LRU_ROWS, :] = (jnp.concatenate(groups, axis=0)
                                        * _gelu_tanh(gate)).astype(lru_ref.dtype)
        yield
    hcarry[...] = carry


def _mlp_stage(x_ref, gdn_ref, lru_s, wog_ref, wol_ref, nw_ref, w1_ref, w2_ref, fnw_ref, o_ref,
               ff_chunk):
    d_ff = w1_ref.shape[1]
    x1 = (x_ref[...]
          + jnp.dot(gdn_ref[...], wog_ref[...], preferred_element_type=_F32)
          + jnp.dot(lru_s[...], wol_ref[...], preferred_element_type=_F32))
    yield
    mb = _rmsnorm(x1, nw_ref[...]).astype(_BF16)
    ff = None
    for c0 in range(0, d_ff, ff_chunk):
        u = jnp.maximum(
            jnp.dot(mb, w1_ref[:, c0:c0 + ff_chunk], preferred_element_type=_F32), 0.0)
        yield
        t = jnp.dot((u * u).astype(_BF16), w2_ref[c0:c0 + ff_chunk, :],
                    preferred_element_type=_F32)
        ff = t if ff is None else ff + t
        yield
    o_ref[...] = _rmsnorm(x1 + ff, fnw_ref[...])


def _mlp_kernel(x_ref, gdn_ref, xr_ref, gate_ref, wog_ref, wol_ref, nw_ref, w1_ref, w2_ref,
                fnw_ref, wlru_ref, blru_ref, ap_ref, o_ref, lru_s, lru_next, hcarry, *, ff_chunk,
                tiles_per_seq):
    i = pl.program_id(0)

    @pl.when(i == 0)
    def _():
        lru_s[...] = jnp.zeros_like(lru_s)

    @pl.when(i % tiles_per_seq == 0)
    def _():
        hcarry[...] = jnp.zeros_like(hcarry)

    _interleave(
        (_mlp_stage(x_ref, gdn_ref, lru_s, wog_ref, wol_ref, nw_ref, w1_ref, w2_ref, fnw_ref,
                    o_ref, ff_chunk), 1),
        (_lru_stage(xr_ref, gate_ref, wlru_ref, blru_ref, ap_ref, hcarry, lru_next), 1))
    lru_s[...] = lru_next[...]


def _mlp(x, gdn, xr, gate, wog, wol, nw, w1, w2, fnw, wlru, blru, ap, *, tm, ff_chunk, seq):
    n_tok, d_model = x.shape
    n_tiles = n_tok // tm
    lru_width = xr.shape[1]
    front_tile = lambda w: pl.BlockSpec((tm, w), lambda i: (jnp.minimum(i, n_tiles - 1), 0))
    back_tile = lambda w: pl.BlockSpec((tm, w), lambda i: (jnp.maximum(i - 1, 0), 0))
    consts = (wog, wol, nw, w1, w2, fnw, wlru, blru, ap)
    return pl.pallas_call(
        functools.partial(_mlp_kernel, ff_chunk=ff_chunk, tiles_per_seq=seq // tm),
        out_shape=jax.ShapeDtypeStruct((n_tok, d_model), _F32),
        grid=(n_tiles + 1,),
        in_specs=[back_tile(d_model), back_tile(gdn.shape[1]), front_tile(lru_width),
                  front_tile(lru_width)] + [_const_spec(c.shape) for c in consts],
        out_specs=back_tile(d_model),
        scratch_shapes=[pltpu.VMEM((tm, lru_width), _BF16),
                        pltpu.VMEM((tm, lru_width), _BF16),
                        pltpu.VMEM((1, lru_width), _F32)],
        compiler_params=pltpu.CompilerParams(
            dimension_semantics=("arbitrary",),
            vmem_limit_bytes=VMEM_LIMIT_BYTES),
        name="outproj_mlp",
    )(x, gdn, xr, gate, *consts)


def _block_diag(w):
    g, i, j = w.shape
    eye = jnp.eye(g, dtype=w.dtype)
    return (eye[:, None, :, None] * w[:, :, None, :]).reshape(g * i, g * j)


def _layer(x, norm_mix_w, w_in, gdn_conv_w, gdn_A_log, gdn_dt_bias, gdn_norm_w,
           lru_conv_w, lru_conv_b, lru_gate_a_w, lru_gate_a_b, lru_gate_x_w, lru_gate_x_b,
           lru_a_param, w_out, norm_mlp_w, w_ff1, w_ff2, final_norm_w, *, seq_tile, tok_tile,
           ff_chunk):
    bsz, seq, d_model = x.shape
    n_heads = gdn_A_log.shape[0]
    gdn_width = n_heads * HEAD_DIM
    lru_width = lru_conv_b.shape[0]
    assert 2 * n_heads <= LANES
    assert seq % seq_tile == 0 and seq_tile % GDN_CHUNK == 0
    assert seq % tok_tile == 0 and tok_tile % LRU_ROWS == 0 and w_ff1.shape[1] % ff_chunk == 0
    assert MXU_TILE % lru_gate_a_w.shape[1] == 0 and lru_width % MXU_TILE == 0

    o_qkv, o_z = 0, 3 * gdn_width
    o_b = o_z + gdn_width
    o_a = o_b + n_heads
    o_lx = o_a + n_heads
    o_lg = o_lx + lru_width
    wconv = jnp.concatenate([w_in[:, o_qkv:o_z], w_in[:, o_lx:o_lg]], axis=1).astype(_BF16)
    wgate = jnp.concatenate([w_in[:, o_z:o_b], w_in[:, o_lg:o_lg + lru_width]], axis=1).astype(_BF16)
    wba = jnp.pad(w_in[:, o_b:o_lx], ((0, 0), (0, LANES - 2 * n_heads))).astype(_BF16)
    cw = jnp.concatenate([gdn_conv_w, lru_conv_w], axis=1).astype(_F32)
    pad_heads = lambda p: jnp.pad(p.astype(_F32), (n_heads, LANES - 2 * n_heads)).reshape(1, LANES)
    wlru = jnp.concatenate([_block_diag(lru_gate_a_w), _block_diag(lru_gate_x_w)], axis=1).astype(_BF16)
    blru = jnp.concatenate([lru_gate_a_b.reshape(1, -1), lru_gate_x_b.reshape(1, -1)], axis=1)

    n_tok = bsz * seq
    x2d = x.reshape(n_tok, d_model)
    gdn, xr, gate = _mixer(
        x2d, norm_mix_w.reshape(1, -1), wconv, wgate, wba, cw, lru_conv_b.reshape(1, -1),
        pad_heads(gdn_A_log), pad_heads(gdn_dt_bias), gdn_norm_w.reshape(1, -1),
        ts=seq_tile, seq=seq)

    out = _mlp(x2d, gdn, xr, gate,
               w_out[:gdn_width].astype(_BF16), w_out[gdn_width:].astype(_BF16),
               norm_mlp_w.reshape(1, -1), w_ff1.astype(_BF16), w_ff2.astype(_BF16),
               final_norm_w.reshape(1, -1), wlru, blru, lru_a_param.reshape(1, -1),
               tm=tok_tile, ff_chunk=ff_chunk, seq=seq)
    return out.reshape(bsz, seq, d_model)


def kernel(x, norm_mix_w, w_in, gdn_conv_w, gdn_A_log, gdn_dt_bias, gdn_norm_w, lru_conv_w, lru_conv_b, lru_gate_a_w, lru_gate_a_b, lru_gate_x_w, lru_gate_x_b, lru_a_param, w_out, norm_mlp_w, w_ff1, w_ff2, final_norm_w):
    assert norm_mix_w.shape[0] == 1, "single-layer stack"
    return _layer(x, norm_mix_w[0], w_in[0], gdn_conv_w[0], gdn_A_log[0], gdn_dt_bias[0],
                  gdn_norm_w[0], lru_conv_w[0], lru_conv_b[0], lru_gate_a_w[0], lru_gate_a_b[0],
                  lru_gate_x_w[0], lru_gate_x_b[0], lru_a_param[0], w_out[0], norm_mlp_w[0],
                  w_ff1[0], w_ff2[0], final_norm_w, seq_tile=256, tok_tile=512, ff_chunk=512)
```

```python
import functools
import math

import jax
import jax.numpy as jnp
from jax import lax
from jax.experimental import pallas as pl
from jax.experimental.pallas import tpu as pltpu

EPS = 1e-6
HEAD_DIM = 128
LANES = 128
SUBLANES = 8
GDN_CHUNK = 64
INV_BASE = 16
LRU_C = 8.0
CONV_WIDTH = 4
CONV_HALO = 8
PROJ_COLS = 256
LRU_ROWS = 256
F32_TINY = 1.1754944e-38
MXU_TILE = 256
VMEM_LIMIT_BYTES = 56 * 1024 * 1024

_BF16 = jnp.bfloat16
_F32 = jnp.float32
_NT_DIMS = (((1,), (1,)), ((), ()))


def _dot(a, b):
    return jnp.dot(a.astype(_BF16), b.astype(_BF16), preferred_element_type=_F32)


def _dot_nt(a, b):
    return lax.dot_general(a.astype(_BF16), b.astype(_BF16), _NT_DIMS,
                           preferred_element_type=_F32)


def _sigmoid(x):
    return 1.0 / (1.0 + jnp.exp(-x))


def _silu(x):
    hx = 0.5 * x
    return hx * jnp.tanh(hx) + hx


def _softplus(x):
    return jnp.maximum(x, 0.0) + jnp.log1p(jnp.exp(-jnp.abs(x)))


def _gelu_tanh(x):
    c = math.sqrt(2.0 / math.pi)
    hx = 0.5 * x
    return hx * jnp.tanh(x * (c + (c * 0.044715) * (x * x))) + hx


def _rmsnorm(x, w):
    return x * lax.rsqrt(jnp.mean(x * x, axis=-1, keepdims=True) + EPS) * w


def _const_spec(shape):
    nd = len(shape)
    return pl.BlockSpec(shape, lambda *_: (0,) * nd, pipeline_mode=pl.Buffered(1))


def _interleave(*stages):
    live = list(stages)
    while live:
        for entry in list(live):
            stage, pieces = entry
            for _ in range(pieces):
                try:
                    next(stage)
                except StopIteration:
                    live.remove(entry)
                    break


def _causal_conv(pbuf, cw_ref, ts):
    pe = pbuf[...]
    pd = pltpu.roll(pe, 1, 0)
    near = cw_ref[3:4, :] * pe + cw_ref[2:3, :] * pd
    far = cw_ref[1:2, :] * pe + cw_ref[0:1, :] * pd
    y = near + pltpu.roll(far, 2, 0)
    return y[CONV_HALO:CONV_HALO + ts, :]


def _pack_chunks(m, lane_chunk):
    n = m.shape[0] // GDN_CHUNK
    out = m[(n - 1) * GDN_CHUNK:, :]
    for c in range(n - 2, -1, -1):
        out = jnp.where(lane_chunk == c, m[c * GDN_CHUNK:(c + 1) * GDN_CHUNK, :], out)
    return out


def _block_diag_of(p, bd_mask):
    pb = p.astype(_BF16)
    return jnp.concatenate([pb] * (bd_mask.shape[0] // GDN_CHUNK), axis=0) * bd_mask


def _packed_dot(x, p_bd):
    return jnp.dot(x.astype(_BF16), p_bd, preferred_element_type=_F32)


def _unit_lower_inverse(a_heads, r, j, bd_mask):
    c = GDN_CHUNK
    eye = (r == j).astype(_F32)
    base = (r // INV_BASE) == (j // INV_BASE)
    d = [jnp.where(base, a, 0.0) for a in a_heads]
    x = [eye - dh for dh in d]
    p = [_packed_dot(dh, _block_diag_of(dh, bd_mask)) for dh in d]
    span = 4
    while span < INV_BASE:
        both = [_packed_dot(jnp.concatenate([xh, ph], axis=0), _block_diag_of(ph, bd_mask))
                for xh, ph in zip(x, p)]
        x = [xh + bh[:c] for xh, bh in zip(x, both)]
        p = [bh[c:] for bh in both]
        span *= 2
    x = [xh + _packed_dot(xh, _block_diag_of(ph, bd_mask)) for xh, ph in zip(x, p)]
    size = INV_BASE
    while size < GDN_CHUNK:
        off = ((r // (2 * size)) == (j // (2 * size))) & ((r // size) != (j // size))
        xe = [_packed_dot(xh, _block_diag_of(jnp.where(off, a, 0.0), bd_mask))
              for xh, a in zip(x, a_heads)]
        x = [xh - _packed_dot(xeh, _block_diag_of(xh, bd_mask)) for xh, xeh in zip(x, xe)]
        size *= 2
    return x


def _gdn_chunk_operands(q_all, k_all, v_all, gb, lhs_s, n_s, o2_s, gam_s):
    n_heads = len(q_all)
    heads = range(n_heads)
    ts = gb.shape[0]
    n_chunks = ts // GDN_CHUNK
    row = lax.broadcasted_iota(jnp.int32, (ts, ts), 0)
    col = lax.broadcasted_iota(jnp.int32, (ts, ts), 1)
    same_chunk = (row // GDN_CHUNK) == (col // GDN_CHUNK)
    bd_mask = jnp.where(same_chunk, 1.0, 0.0).astype(_BF16)
    r = lax.broadcasted_iota(jnp.int32, (GDN_CHUNK, ts), 0)
    lane = lax.broadcasted_iota(jnp.int32, (GDN_CHUNK, ts), 1)
    lane_chunk = lane // GDN_CHUNK
    j = lane % GDN_CHUNK
    causal = r >= j
    strict = r > j

    ones_tri = jnp.where(same_chunk & (row >= col), 1.0, 0.0).astype(_BF16)
    gb_hi = gb.astype(_BF16)
    gb_r1 = gb - gb_hi.astype(_F32)
    gb_mid = gb_r1.astype(_BF16)
    gb_lo = (gb_r1 - gb_mid.astype(_F32)).astype(_BF16)
    gcum = (jnp.dot(ones_tri, gb_hi, preferred_element_type=_F32)
            + jnp.dot(ones_tri, gb_mid, preferred_element_type=_F32)
            + jnp.dot(ones_tri, gb_lo, preferred_element_type=_F32))
    gcum_t = gcum.T

    gcol_all, a_all, rhs_all, qk_all, qg_all = [], [], [], [], []
    for hd in heads:
        beta = gb[:, hd:hd + 1]
        gcol = gcum[:, n_heads + hd:n_heads + hd + 1]
        grow = gcum_t[n_heads + hd:n_heads + hd + 1, :]
        gpk = _pack_chunks(jnp.broadcast_to(gcol, (ts, ts)), lane_chunk)
        decay = jnp.where(causal, jnp.exp(jnp.where(causal, gpk - grow, 0.0)), 0.0)
        q, k, v = q_all[hd], k_all[hd], v_all[hd]
        egc = jnp.exp(gcol)
        kb = k * beta
        gcol_all.append(gcol)
        a_all.append(jnp.where(strict, _pack_chunks(_dot_nt(kb, k), lane_chunk) * decay, 0.0))
        rhs_all.append(jnp.concatenate([v * beta, kb * egc], axis=1))
        qk_all.append(_pack_chunks(_dot_nt(q, k), lane_chunk) * decay)
        qg_all.append(q * egc)
    t_all = _unit_lower_inverse(a_all, r, j, bd_mask)
    sol_all = [jnp.dot(_block_diag_of(t, bd_mask), rhs.astype(_BF16), preferred_element_type=_F32)
               for t, rhs in zip(t_all, rhs_all)]

    for c in range(n_chunks):
        r0 = c * GDN_CHUNK
        r1 = r0 + GDN_CHUNK
        for hd in heads:
            idx = hd * n_chunks + c
            sol = sol_all[hd][r0:r1]
            g_last = gcol_all[hd][r1 - 1:r1, :]
            k_dec = k_all[hd][r0:r1] * jnp.exp(g_last - gcol_all[hd][r0:r1])
            kd_sol = _dot(k_dec.T, sol)
            qk_sol = _dot(qk_all[hd][:, r0:r1], sol)
            lhs_s[idx] = jnp.concatenate(
                [kd_sol[:, HEAD_DIM:], qg_all[hd][r0:r1] - qk_sol[:, HEAD_DIM:]],
                axis=0).astype(lhs_s.dtype)
            n_s[idx] = kd_sol[:, :HEAD_DIM]
            o2_s[idx] = qk_sol[:, :HEAD_DIM]
            gam_s[idx] = jnp.broadcast_to(jnp.exp(g_last), gam_s.shape[1:])


def _gdn_recurrence_stage(lhs_s, n_s, o2_s, gam_s, z_ref, gnw_ref, state, gdn_ref):
    n_heads = state.shape[0]
    heads = range(n_heads)
    n_chunks = lhs_s.shape[0] // n_heads
    s_all = [state[hd] for hd in heads]
    outs = [[] for _ in heads]
    for c in range(n_chunks):
        for hd in heads:
            idx = hd * n_chunks + c
            s = s_all[hd]
            r = jnp.dot(lhs_s[idx], s.astype(_BF16), preferred_element_type=_F32)
            outs[hd].append(r[HEAD_DIM:] + o2_s[idx])
            s_all[hd] = s * gam_s[idx][0:1, :] - r[:HEAD_DIM] + n_s[idx]
        yield
    for hd in heads:
        lo = hd * HEAD_DIM
        state[hd] = s_all[hd]
        z = z_ref[:, lo:lo + HEAD_DIM]
        o = _rmsnorm(jnp.concatenate(outs[hd], axis=0), gnw_ref[...]) * _silu(z)
        gdn_ref[:, lo:lo + HEAD_DIM] = o.astype(gdn_ref.dtype)
        yield


def _project_stage(x_ref, nw_ref, wconv_ref, wgate_ref, wba_ref, pbuf, out):
    ts = x_ref.shape[0]
    hb = _rmsnorm(x_ref[...], nw_ref[...]).astype(_BF16)
    yield
    for c0 in range(0, wconv_ref.shape[1], PROJ_COLS):
        pbuf[CONV_HALO:CONV_HALO + ts, c0:c0 + PROJ_COLS] = jnp.dot(
            hb, wconv_ref[:, c0:c0 + PROJ_COLS], preferred_element_type=_F32)
        yield
    parts = []
    for c0 in range(0, wgate_ref.shape[1], PROJ_COLS):
        parts.append(jnp.dot(hb, wgate_ref[:, c0:c0 + PROJ_COLS], preferred_element_type=_F32))
        yield
    out["pg"] = jnp.concatenate(parts, axis=1)
    out["pba"] = jnp.dot(hb, wba_ref[...], preferred_element_type=_F32)


def _mixer_kernel(x_ref, nw_ref, wconv_ref, wgate_ref, wba_ref, cw_ref, lcb_ref,
                  alog_ref, dtb_ref, gnw_ref, wlru_ref, blru_ref, ap_ref,
                  gdn_ref, lru_ref,
                  pbuf, z_s, lhs_s, n_s, o2_s, gam_s, state, xr_s, gate_s, hcarry, *,
                  tiles_per_seq):
    j = pl.program_id(0)
    ts = x_ref.shape[0]
    gdn_width = gdn_ref.shape[1]
    n_heads = gdn_width // HEAD_DIM

    @pl.when(j == 0)
    def _():
        for ref in (z_s, lhs_s, n_s, o2_s, gam_s, xr_s, gate_s):
            ref[...] = jnp.zeros_like(ref)

    @pl.when(j % tiles_per_seq == 0)
    def _():
        pbuf[0:CONV_HALO, :] = jnp.zeros((CONV_HALO, pbuf.shape[1]), _F32)

    @pl.when((j == 0) | (j % tiles_per_seq == 1 % tiles_per_seq))
    def _():
        state[...] = jnp.zeros_like(state)
        hcarry[...] = jnp.zeros_like(hcarry)

    proj = {}
    _interleave(
        (_project_stage(x_ref, nw_ref, wconv_ref, wgate_ref, wba_ref, pbuf, proj), 1),
        (_lru_stage(xr_s, gate_s, wlru_ref, blru_ref, ap_ref, hcarry, lru_ref), 1),
        (_gdn_recurrence_stage(lhs_s, n_s, o2_s, gam_s, z_s, gnw_ref, state, gdn_ref), 1))

    y = _causal_conv(pbuf, cw_ref, ts)
    pbuf[0:CONV_HALO, :] = pbuf[ts:ts + CONV_HALO, :]

    pba = proj["pba"]
    lane = lax.broadcasted_iota(jnp.int32, pba.shape, 1)
    gb = jnp.where(lane < n_heads, _sigmoid(pba),
                   -jnp.exp(alog_ref[...]) * _softplus(pba + dtb_ref[...]))

    qkv = y[:, :3 * gdn_width]
    qkv = _silu(qkv)
    scale = HEAD_DIM ** -0.5
    q_all, k_all, v_all = [], [], []
    for hd in range(n_heads):
        lo = hd * HEAD_DIM
        qh = qkv[:, lo:lo + HEAD_DIM]
        kh = qkv[:, gdn_width + lo:gdn_width + lo + HEAD_DIM]
        q_all.append(qh * (lax.rsqrt(jnp.sum(qh * qh, axis=-1, keepdims=True) + EPS) * scale))
        k_all.append(kh * lax.rsqrt(jnp.sum(kh * kh, axis=-1, keepdims=True) + EPS))
        v_all.append(qkv[:, 2 * gdn_width + lo:2 * gdn_width + lo + HEAD_DIM])

    _gdn_chunk_operands(q_all, k_all, v_all, gb, lhs_s, n_s, o2_s, gam_s)
    pg = proj["pg"]
    z_s[...] = pg[:, :gdn_width]

    xr_s[...] = (y[:, 3 * gdn_width:] + lcb_ref[...]).astype(xr_s.dtype)
    gate_s[...] = pg[:, gdn_width:].astype(gate_s.dtype)


def _mixer(x, nw, wconv, wgate, wba, cw, lcb, alog, dtb, gnw, wlru, blru, ap, *, ts, seq):
    n_tok, d_model = x.shape
    lru_width = lcb.shape[1]
    gdn_width = wgate.shape[1] - lru_width
    n_heads = gdn_width // HEAD_DIM
    n_tiles = n_tok // ts
    n_ops = n_heads * (ts // GDN_CHUNK)
    consts = (nw, wconv, wgate, wba, cw, lcb, alog, dtb, gnw, wlru, blru, ap)
    front_tile = lambda w: pl.BlockSpec((ts, w), lambda j: (jnp.minimum(j, n_tiles - 1), 0))
    back_tile = lambda w: pl.BlockSpec((ts, w), lambda j: (jnp.maximum(j - 1, 0), 0))
    return pl.pallas_call(
        functools.partial(_mixer_kernel, tiles_per_seq=seq // ts),
        out_shape=(jax.ShapeDtypeStruct((n_tok, gdn_width), _BF16),
                   jax.ShapeDtypeStruct((n_tok, lru_width), _BF16)),
        grid=(n_tiles + 1,),
        in_specs=[front_tile(d_model)] + [_const_spec(c.shape) for c in consts],
        out_specs=(back_tile(gdn_width), back_tile(lru_width)),
        scratch_shapes=[pltpu.VMEM((ts + CONV_HALO, wconv.shape[1]), _F32),
                        pltpu.VMEM((ts, gdn_width), _F32),
                        pltpu.VMEM((n_ops, HEAD_DIM + GDN_CHUNK, HEAD_DIM), _BF16),
                        pltpu.VMEM((n_ops, HEAD_DIM, HEAD_DIM), _F32),
                        pltpu.VMEM((n_ops, GDN_CHUNK, HEAD_DIM), _F32),
                        pltpu.VMEM((n_ops, SUBLANES, LANES), _F32),
                        pltpu.VMEM((n_heads, HEAD_DIM, HEAD_DIM), _F32),
                        pltpu.VMEM((ts, lru_width), _BF16),
                        pltpu.VMEM((ts, lru_width), _BF16),
                        pltpu.VMEM((1, lru_width), _F32)],
        compiler_params=pltpu.CompilerParams(
            dimension_semantics=("arbitrary",),
            vmem_limit_bytes=VMEM_LIMIT_BYTES),
        name="mixer",
    )(x, *consts)


def _block_diag_dot(xb, w_ref, col0):
    k = xb.shape[1]
    return jnp.concatenate(
        [jnp.dot(xb[:, t:t + MXU_TILE], w_ref[t:t + MXU_TILE, col0 + t:col0 + t + MXU_TILE],
                 preferred_element_type=_F32) for t in range(0, k, MXU_TILE)], axis=1)


def _lru_stage(xr_ref, gate_ref, wlru_ref, blru_ref, ap_ref, hcarry, lru_ref):
    ts, lru_width = xr_ref.shape
    nsp = _softplus(-ap_ref[...])
    row = lax.broadcasted_iota(jnp.int32, (LRU_ROWS // SUBLANES, SUBLANES, lru_width), 1)
    carry = hcarry[...]
    for b0 in range(0, ts, LRU_ROWS):
        xb = xr_ref[b0:b0 + LRU_ROWS, :]
        xr = xb.astype(_F32)
        r = _sigmoid(_block_diag_dot(xb, wlru_ref, 0) + blru_ref[:, :lru_width])
        log_a = -LRU_C * r * nsp
        a = jnp.exp(log_a)
        yield
        i = _sigmoid(_block_diag_dot(xb, wlru_ref, lru_width) + blru_ref[:, lru_width:])
        v = jnp.maximum(1.0 - a * a, 0.0)
        b = v * lax.rsqrt(jnp.maximum(v, F32_TINY)) * (i * xr)
        yield
        a = a.reshape(LRU_ROWS // SUBLANES, SUBLANES, lru_width)
        b = b.reshape(LRU_ROWS // SUBLANES, SUBLANES, lru_width)
        d = 1
        while d < SUBLANES:
            keep = row >= d
            a_sh = jnp.where(keep, pltpu.roll(a, d, 1), 1.0)
            b_sh = jnp.where(keep, pltpu.roll(b, d, 1), 0.0)
            b = a * b_sh + b
            a = a * a_sh
            d *= 2
        a = a.reshape(LRU_ROWS, lru_width)
        b = b.reshape(LRU_ROWS, lru_width)
        yield
        groups = []
        for r0 in range(0, LRU_ROWS, SUBLANES):
            hg = b[r0:r0 + SUBLANES] + a[r0:r0 + SUBLANES] * carry
            carry = hg[SUBLANES - 1:SUBLANES]
            groups.append(hg)
        gate = gate_ref[b0:b0 + LRU_ROWS, :].astype(_F32)
        lru_ref[b0:b0 + LRU_ROWS, :] = (jnp.concatenate(groups, axis=0)
                                        * _gelu_tanh(gate)).astype(lru_ref.dtype)
        yield
    hcarry[...] = carry


def _mlp_kernel(x_ref, gdn_ref, lru_ref, wog_ref, wol_ref, nw_ref, w1_ref, w2_ref, fnw_ref,
                o_ref, *, ff_chunk):
    d_ff = w1_ref.shape[1]
    x1 = (x_ref[...]
          + jnp.dot(gdn_ref[...], wog_ref[...], preferred_element_type=_F32)
          + jnp.dot(lru_ref[...], wol_ref[...], preferred_element_type=_F32))
    mb = _rmsnorm(x1, nw_ref[...]).astype(_BF16)
    ff = None
    for c0 in range(0, d_ff, ff_chunk):
        u = jnp.maximum(
            jnp.dot(mb, w1_ref[:, c0:c0 + ff_chunk], preferred_element_type=_F32), 0.0)
        t = jnp.dot((u * u).astype(_BF16), w2_ref[c0:c0 + ff_chunk, :],
                    preferred_element_type=_F32)
        ff = t if ff is None else ff + t
    o_ref[...] = _rmsnorm(x1 + ff, fnw_ref[...])


def _mlp(x, gdn, lru, wog, wol, nw, w1, w2, fnw, *, tm, ff_chunk):
    n_tok, d_model = x.shape
    tile = lambda w: pl.BlockSpec((tm, w), lambda i: (i, 0))
    consts = (wog, wol, nw, w1, w2, fnw)
    return pl.pallas_call(
        functools.partial(_mlp_kernel, ff_chunk=ff_chunk),
        out_shape=jax.ShapeDtypeStruct((n_tok, d_model), _F32),
        grid=(n_tok // tm,),
        in_specs=[tile(d_model), tile(gdn.shape[1]), tile(lru.shape[1])]
                 + [_const_spec(c.shape) for c in consts],
        out_specs=tile(d_model),
        compiler_params=pltpu.CompilerParams(
            dimension_semantics=("parallel",),
            vmem_limit_bytes=VMEM_LIMIT_BYTES),
        name="outproj_mlp",
    )(x, gdn, lru, *consts)


def _block_diag(w):
    g, i, j = w.shape
    eye = jnp.eye(g, dtype=w.dtype)
    return (eye[:, None, :, None] * w[:, :, None, :]).reshape(g * i, g * j)


def _layer(x, norm_mix_w, w_in, gdn_conv_w, gdn_A_log, gdn_dt_bias, gdn_norm_w,
           lru_conv_w, lru_conv_b, lru_gate_a_w, lru_gate_a_b, lru_gate_x_w, lru_gate_x_b,
           lru_a_param, w_out, norm_mlp_w, w_ff1, w_ff2, final_norm_w, *, seq_tile, tok_tile,
           ff_chunk):
    bsz, seq, d_model = x.shape
    n_heads = gdn_A_log.shape[0]
    gdn_width = n_heads * HEAD_DIM
    lru_width = lru_conv_b.shape[0]
    assert 2 * n_heads <= LANES
    assert seq % seq_tile == 0 and seq_tile % GDN_CHUNK == 0
    assert seq_tile % LRU_ROWS == 0
    assert (bsz * seq) % tok_tile == 0 and w_ff1.shape[1] % ff_chunk == 0
    assert MXU_TILE % lru_gate_a_w.shape[1] == 0 and lru_width % MXU_TILE == 0

    o_qkv, o_z = 0, 3 * gdn_width
    o_b = o_z + gdn_width
    o_a = o_b + n_heads
    o_lx = o_a + n_heads
    o_lg = o_lx + lru_width
    wconv = jnp.concatenate([w_in[:, o_qkv:o_z], w_in[:, o_lx:o_lg]], axis=1).astype(_BF16)
    wgate = jnp.concatenate([w_in[:, o_z:o_b], w_in[:, o_lg:o_lg + lru_width]], axis=1).astype(_BF16)
    wba = jnp.pad(w_in[:, o_b:o_lx], ((0, 0), (0, LANES - 2 * n_heads))).astype(_BF16)
    cw = jnp.concatenate([gdn_conv_w, lru_conv_w], axis=1).astype(_F32)
    pad_heads = lambda p: jnp.pad(p.astype(_F32), (n_heads, LANES - 2 * n_heads)).reshape(1, LANES)
    wlru = jnp.concatenate([_block_diag(lru_gate_a_w), _block_diag(lru_gate_x_w)], axis=1).astype(_BF16)
    blru = jnp.concatenate([lru_gate_a_b.reshape(1, -1), lru_gate_x_b.reshape(1, -1)], axis=1)

    n_tok = bsz * seq
    x2d = x.reshape(n_tok, d_model)
    gdn, lru = _mixer(
        x2d, norm_mix_w.reshape(1, -1), wconv, wgate, wba, cw, lru_conv_b.reshape(1, -1),
        pad_heads(gdn_A_log), pad_heads(gdn_dt_bias), gdn_norm_w.reshape(1, -1),
        wlru, blru, lru_a_param.reshape(1, -1), ts=seq_tile, seq=seq)

    out = _mlp(x2d, gdn, lru,
               w_out[:gdn_width].astype(_BF16), w_out[gdn_width:].astype(_BF16),
               norm_mlp_w.reshape(1, -1), w_ff1.astype(_BF16), w_ff2.astype(_BF16),
               final_norm_w.reshape(1, -1), tm=tok_tile, ff_chunk=ff_chunk)
    return out.reshape(bsz, seq, d_model)


def kernel(x, norm_mix_w, w_in, gdn_conv_w, gdn_A_log, gdn_dt_bias, gdn_norm_w, lru_conv_w, lru_conv_b, lru_gate_a_w, lru_gate_a_b, lru_gate_x_w, lru_gate_x_b, lru_a_param, w_out, norm_mlp_w, w_ff1, w_ff2, final_norm_w):
    assert norm_mix_w.shape[0] == 1, "single-layer stack"
    return _layer(x, norm_mix_w[0], w_in[0], gdn_conv_w[0], gdn_A_log[0], gdn_dt_bias[0],
                  gdn_norm_w[0], lru_conv_w[0], lru_conv_b[0], lru_gate_a_w[0], lru_gate_a_b[0],
                  lru_gate_x_w[0], lru_gate_x_b[0], lru_a_param[0], w_out[0], norm_mlp_w[0],
                  w_ff1[0], w_ff2[0], final_norm_w, seq_tile=256, tok_tile=1024, ff_chunk=2048)
```

```python
import functools
import math

import jax
import jax.numpy as jnp
from jax import lax
from jax.experimental import pallas as pl
from jax.experimental.pallas import tpu as pltpu

EPS = 1e-6
HEAD_DIM = 128
LANES = 128
SUBLANES = 8
GDN_CHUNK = 64
INV_BASE = 16
LRU_C = 8.0
CONV_WIDTH = 4
CONV_HALO = 8
PROJ_COLS = 256
LRU_ROWS = 256
F32_TINY = 1.1754944e-38
MXU_TILE = 256
VMEM_LIMIT_BYTES = 56 * 1024 * 1024

_BF16 = jnp.bfloat16
_F32 = jnp.float32
_NT_DIMS = (((1,), (1,)), ((), ()))


def _dot(a, b):
    return jnp.dot(a.astype(_BF16), b.astype(_BF16), preferred_element_type=_F32)


def _dot_nt(a, b):
    return lax.dot_general(a.astype(_BF16), b.astype(_BF16), _NT_DIMS,
                           preferred_element_type=_F32)


def _sigmoid(x):
    return 1.0 / (1.0 + jnp.exp(-x))


def _silu(x):
    hx = 0.5 * x
    return hx * jnp.tanh(hx) + hx


def _softplus(x):
    return jnp.maximum(x, 0.0) + jnp.log1p(jnp.exp(-jnp.abs(x)))


def _gelu_tanh(x):
    c = math.sqrt(2.0 / math.pi)
    hx = 0.5 * x
    return hx * jnp.tanh(x * (c + (c * 0.044715) * (x * x))) + hx


def _rmsnorm(x, w):
    return x * lax.rsqrt(jnp.mean(x * x, axis=-1, keepdims=True) + EPS) * w


def _const_spec(shape):
    nd = len(shape)
    return pl.BlockSpec(shape, lambda *_: (0,) * nd, pipeline_mode=pl.Buffered(1))


def _interleave(*stages):
    live = list(stages)
    while live:
        for entry in list(live):
            stage, pieces = entry
            for _ in range(pieces):
                try:
                    next(stage)
                except StopIteration:
                    live.remove(entry)
                    break


def _causal_conv(pbuf, cw_ref, ts):
    pe = pbuf[...]
    pd = pltpu.roll(pe, 1, 0)
    near = cw_ref[3:4, :] * pe + cw_ref[2:3, :] * pd
    far = cw_ref[1:2, :] * pe + cw_ref[0:1, :] * pd
    y = near + pltpu.roll(far, 2, 0)
    return y[CONV_HALO:CONV_HALO + ts, :]


def _pack_chunks(m, lane_chunk):
    n = m.shape[0] // GDN_CHUNK
    out = m[(n - 1) * GDN_CHUNK:, :]
    for c in range(n - 2, -1, -1):
        out = jnp.where(lane_chunk == c, m[c * GDN_CHUNK:(c + 1) * GDN_CHUNK, :], out)
    return out


def _block_diag_of(p, bd_mask):
    pb = p.astype(_BF16)
    return jnp.concatenate([pb] * (bd_mask.shape[0] // GDN_CHUNK), axis=0) * bd_mask


def _packed_dot(x, p_bd):
    return jnp.dot(x.astype(_BF16), p_bd, preferred_element_type=_F32)


def _unit_lower_inverse(a_heads, r, j, bd_mask):
    c = GDN_CHUNK
    eye = (r == j).astype(_F32)
    base = (r // INV_BASE) == (j // INV_BASE)
    d = [jnp.where(base, a, 0.0) for a in a_heads]
    x = [eye - dh for dh in d]
    p = [_packed_dot(dh, _block_diag_of(dh, bd_mask)) for dh in d]
    span = 4
    while span < INV_BASE:
        both = [_packed_dot(jnp.concatenate([xh, ph], axis=0), _block_diag_of(ph, bd_mask))
                for xh, ph in zip(x, p)]
        x = [xh + bh[:c] for xh, bh in zip(x, both)]
        p = [bh[c:] for bh in both]
        span *= 2
    x = [xh + _packed_dot(xh, _block_diag_of(ph, bd_mask)) for xh, ph in zip(x, p)]
    size = INV_BASE
    while size < GDN_CHUNK:
        off = ((r // (2 * size)) == (j // (2 * size))) & ((r // size) != (j // size))
        xe = [_packed_dot(xh, _block_diag_of(jnp.where(off, a, 0.0), bd_mask))
              for xh, a in zip(x, a_heads)]
        x = [xh - _packed_dot(xeh, _block_diag_of(xh, bd_mask)) for xh, xeh in zip(x, xe)]
        size *= 2
    return x


def _gdn_chunk_operands(q_all, k_all, v_all, gb, lhs_s, n_s, o2_s, gam_s):
    n_heads = len(q_all)
    heads = range(n_heads)
    ts = gb.shape[0]
    n_chunks = ts // GDN_CHUNK
    row = lax.broadcasted_iota(jnp.int32, (ts, ts), 0)
    col = lax.broadcasted_iota(jnp.int32, (ts, ts), 1)
    same_chunk = (row // GDN_CHUNK) == (col // GDN_CHUNK)
    bd_mask = jnp.where(same_chunk, 1.0, 0.0).astype(_BF16)
    r = lax.broadcasted_iota(jnp.int32, (GDN_CHUNK, ts), 0)
    lane = lax.broadcasted_iota(jnp.int32, (GDN_CHUNK, ts), 1)
    lane_chunk = lane // GDN_CHUNK
    j = lane % GDN_CHUNK
    causal = r >= j
    strict = r > j

    ones_tri = jnp.where(same_chunk & (row >= col), 1.0, 0.0).astype(_BF16)
    gb_hi = gb.astype(_BF16)
    gb_r1 = gb - gb_hi.astype(_F32)
    gb_mid = gb_r1.astype(_BF16)
    gb_lo = (gb_r1 - gb_mid.astype(_F32)).astype(_BF16)
    gcum = (jnp.dot(ones_tri, gb_hi, preferred_element_type=_F32)
            + jnp.dot(ones_tri, gb_mid, preferred_element_type=_F32)
            + jnp.dot(ones_tri, gb_lo, preferred_element_type=_F32))
    gcum_t = gcum.T

    gcol_all, a_all, rhs_all, qk_all, qg_all = [], [], [], [], []
    for hd in heads:
        beta = gb[:, hd:hd + 1]
        gcol = gcum[:, n_heads + hd:n_heads + hd + 1]
        grow = gcum_t[n_heads + hd:n_heads + hd + 1, :]
        gpk = _pack_chunks(jnp.broadcast_to(gcol, (ts, ts)), lane_chunk)
        decay = jnp.where(causal, jnp.exp(jnp.where(causal, gpk - grow, 0.0)), 0.0)
        q, k, v = q_all[hd], k_all[hd], v_all[hd]
        egc = jnp.exp(gcol)
        kb = k * beta
        gcol_all.append(gcol)
        a_all.append(jnp.where(strict, _pack_chunks(_dot_nt(kb, k), lane_chunk) * decay, 0.0))
        rhs_all.append(jnp.concatenate([v * beta, kb * egc], axis=1))
        qk_all.append(_pack_chunks(_dot_nt(q, k), lane_chunk) * decay)
        qg_all.append(q * egc)
    t_all = _unit_lower_inverse(a_all, r, j, bd_mask)
    sol_all = [jnp.dot(_block_diag_of(t, bd_mask), rhs.astype(_BF16), preferred_element_type=_F32)
               for t, rhs in zip(t_all, rhs_all)]

    for c in range(n_chunks):
        r0 = c * GDN_CHUNK
        r1 = r0 + GDN_CHUNK
        for hd in heads:
            idx = hd * n_chunks + c
            sol = sol_all[hd][r0:r1]
            g_last = gcol_all[hd][r1 - 1:r1, :]
            k_dec = k_all[hd][r0:r1] * jnp.exp(g_last - gcol_all[hd][r0:r1])
            kd_sol = _dot(k_dec.T, sol)
            qk_sol = _dot(qk_all[hd][:, r0:r1], sol)
            lhs_s[idx] = jnp.concatenate(
                [kd_sol[:, HEAD_DIM:], qg_all[hd][r0:r1] - qk_sol[:, HEAD_DIM:]],
                axis=0).astype(lhs_s.dtype)
            n_s[idx] = kd_sol[:, :HEAD_DIM]
            o2_s[idx] = qk_sol[:, :HEAD_DIM]
            gam_s[idx] = jnp.broadcast_to(jnp.exp(g_last), gam_s.shape[1:])


def _gdn_recurrence_stage(lhs_s, n_s, o2_s, gam_s, z_ref, gnw_ref, state, gdn_ref):
    n_heads = state.shape[0]
    heads = range(n_heads)
    n_chunks = lhs_s.shape[0] // n_heads
    s_all = [state[hd] for hd in heads]
    outs = [[] for _ in heads]
    for c in range(n_chunks):
        for hd in heads:
            idx = hd * n_chunks + c
            s = s_all[hd]
            r = jnp.dot(lhs_s[idx], s.astype(_BF16), preferred_element_type=_F32)
            outs[hd].append(r[HEAD_DIM:] + o2_s[idx])
            s_all[hd] = s * gam_s[idx][0:1, :] - r[:HEAD_DIM] + n_s[idx]
        yield
    for hd in heads:
        lo = hd * HEAD_DIM
        state[hd] = s_all[hd]
        z = z_ref[:, lo:lo + HEAD_DIM]
        o = _rmsnorm(jnp.concatenate(outs[hd], axis=0), gnw_ref[...]) * _silu(z)
        gdn_ref[:, lo:lo + HEAD_DIM] = o.astype(gdn_ref.dtype)
        yield


def _project_stage(x_ref, nw_ref, wconv_ref, wgate_ref, wba_ref, pbuf, out):
    ts = x_ref.shape[0]
    hb = _rmsnorm(x_ref[...], nw_ref[...]).astype(_BF16)
    yield
    for c0 in range(0, wconv_ref.shape[1], PROJ_COLS):
        pbuf[CONV_HALO:CONV_HALO + ts, c0:c0 + PROJ_COLS] = jnp.dot(
            hb, wconv_ref[:, c0:c0 + PROJ_COLS], preferred_element_type=_F32)
        yield
    parts = []
    for c0 in range(0, wgate_ref.shape[1], PROJ_COLS):
        parts.append(jnp.dot(hb, wgate_ref[:, c0:c0 + PROJ_COLS], preferred_element_type=_F32))
        yield
    out["pg"] = jnp.concatenate(parts, axis=1)
    out["pba"] = jnp.dot(hb, wba_ref[...], preferred_element_type=_F32)


def _mixer_kernel(x_ref, nw_ref, wconv_ref, wgate_ref, wba_ref, cw_ref, lcb_ref,
                  alog_ref, dtb_ref, gnw_ref, wlru_ref, blru_ref, ap_ref,
                  gdn_ref, lru_ref,
                  pbuf, z_s, lhs_s, n_s, o2_s, gam_s, state, xr_s, gate_s, hcarry, *,
                  tiles_per_seq):
    j = pl.program_id(0)
    ts = x_ref.shape[0]
    gdn_width = gdn_ref.shape[1]
    n_heads = gdn_width // HEAD_DIM

    @pl.when(j == 0)
    def _():
        for ref in (z_s, lhs_s, n_s, o2_s, gam_s, xr_s, gate_s):
            ref[...] = jnp.zeros_like(ref)

    @pl.when(j % tiles_per_seq == 0)
    def _():
        pbuf[0:CONV_HALO, :] = jnp.zeros((CONV_HALO, pbuf.shape[1]), _F32)

    @pl.when((j == 0) | (j % tiles_per_seq == 1 % tiles_per_seq))
    def _():
        state[...] = jnp.zeros_like(state)
        hcarry[...] = jnp.zeros_like(hcarry)

    proj = {}
    _interleave(
        (_project_stage(x_ref, nw_ref, wconv_ref, wgate_ref, wba_ref, pbuf, proj), 1),
        (_lru_stage(xr_s, gate_s, wlru_ref, blru_ref, ap_ref, hcarry, lru_ref), 1),
        (_gdn_recurrence_stage(lhs_s, n_s, o2_s, gam_s, z_s, gnw_ref, state, gdn_ref), 1))

    y = _causal_conv(pbuf, cw_ref, ts)
    pbuf[0:CONV_HALO, :] = pbuf[ts:ts + CONV_HALO, :]

    pba = proj["pba"]
    lane = lax.broadcasted_iota(jnp.int32, pba.shape, 1)
    gb = jnp.where(lane < n_heads, _sigmoid(pba),
                   -jnp.exp(alog_ref[...]) * _softplus(pba + dtb_ref[...]))

    qkv = y[:, :3 * gdn_width]
    qkv = _silu(qkv)
    scale = HEAD_DIM ** -0.5
    q_all, k_all, v_all = [], [], []
    for hd in range(n_heads):
        lo = hd * HEAD_DIM
        qh = qkv[:, lo:lo + HEAD_DIM]
        kh = qkv[:, gdn_width + lo:gdn_width + lo + HEAD_DIM]
        q_all.append(qh * (lax.rsqrt(jnp.sum(qh * qh, axis=-1, keepdims=True) + EPS) * scale))
        k_all.append(kh * lax.rsqrt(jnp.sum(kh * kh, axis=-1, keepdims=True) + EPS))
        v_all.append(qkv[:, 2 * gdn_width + lo:2 * gdn_width + lo + HEAD_DIM])

    _gdn_chunk_operands(q_all, k_all, v_all, gb, lhs_s, n_s, o2_s, gam_s)
    pg = proj["pg"]
    z_s[...] = pg[:, :gdn_width]

    xr_s[...] = (y[:, 3 * gdn_width:] + lcb_ref[...]).astype(xr_s.dtype)
    gate_s[...] = pg[:, gdn_width:].astype(gate_s.dtype)


def _mixer(x, nw, wconv, wgate, wba, cw, lcb, alog, dtb, gnw, wlru, blru, ap, *, ts, seq):
    n_tok, d_model = x.shape
    lru_width = lcb.shape[1]
    gdn_width = wgate.shape[1] - lru_width
    n_heads = gdn_width // HEAD_DIM
    n_tiles = n_tok // ts
    n_ops = n_heads * (ts // GDN_CHUNK)
    consts = (nw, wconv, wgate, wba, cw, lcb, alog, dtb, gnw, wlru, blru, ap)
    front_tile = lambda w: pl.BlockSpec((ts, w), lambda j: (jnp.minimum(j, n_tiles - 1), 0))
    back_tile = lambda w: pl.BlockSpec((ts, w), lambda j: (jnp.maximum(j - 1, 0), 0))
    return pl.pallas_call(
        functools.partial(_mixer_kernel, tiles_per_seq=seq // ts),
        out_shape=(jax.ShapeDtypeStruct((n_tok, gdn_width), _BF16),
                   jax.ShapeDtypeStruct((n_tok, lru_width), _BF16)),
        grid=(n_tiles + 1,),
        in_specs=[front_tile(d_model)] + [_const_spec(c.shape) for c in consts],
        out_specs=(back_tile(gdn_width), back_tile(lru_width)),
        scratch_shapes=[pltpu.VMEM((ts + CONV_HALO, wconv.shape[1]), _F32),
                        pltpu.VMEM((ts, gdn_width), _F32),
                        pltpu.VMEM((n_ops, HEAD_DIM + GDN_CHUNK, HEAD_DIM), _BF16),
                        pltpu.VMEM((n_ops, HEAD_DIM, HEAD_DIM), _F32),
                        pltpu.VMEM((n_ops, GDN_CHUNK, HEAD_DIM), _F32),
                        pltpu.VMEM((n_ops, SUBLANES, LANES), _F32),
                        pltpu.VMEM((n_heads, HEAD_DIM, HEAD_DIM), _F32),
                        pltpu.VMEM((ts, lru_width), _BF16),
                        pltpu.VMEM((ts, lru_width), _BF16),
                        pltpu.VMEM((1, lru_width), _F32)],
        compiler_params=pltpu.CompilerParams(
            dimension_semantics=("arbitrary",),
            vmem_limit_bytes=VMEM_LIMIT_BYTES),
        name="mixer",
    )(x, *consts)


def _block_diag_dot(xb, w_ref, col0):
    k = xb.shape[1]
    return jnp.concatenate(
        [jnp.dot(xb[:, t:t + MXU_TILE], w_ref[t:t + MXU_TILE, col0 + t:col0 + t + MXU_TILE],
                 preferred_element_type=_F32) for t in range(0, k, MXU_TILE)], axis=1)


def _lru_stage(xr_ref, gate_ref, wlru_ref, blru_ref, ap_ref, hcarry, lru_ref):
    ts, lru_width = xr_ref.shape
    nsp = _softplus(-ap_ref[...])
    row = lax.broadcasted_iota(jnp.int32, (LRU_ROWS // SUBLANES, SUBLANES, lru_width), 1)
    carry = hcarry[...]
    for b0 in range(0, ts, LRU_ROWS):
        xb = xr_ref[b0:b0 + LRU_ROWS, :]
        xr = xb.astype(_F32)
        r = _sigmoid(_block_diag_dot(xb, wlru_ref, 0) + blru_ref[:, :lru_width])
        log_a = -LRU_C * r * nsp
        a = jnp.exp(log_a)
        yield
        i = _sigmoid(_block_diag_dot(xb, wlru_ref, lru_width) + blru_ref[:, lru_width:])
        v = jnp.maximum(1.0 - a * a, 0.0)
        b = v * lax.rsqrt(jnp.maximum(v, F32_TINY)) * (i * xr)
        yield
        a = a.reshape(LRU_ROWS // SUBLANES, SUBLANES, lru_width)
        b = b.reshape(LRU_ROWS // SUBLANES, SUBLANES, lru_width)
        d = 1
        while d < SUBLANES:
            keep = row >= d
            a_sh = jnp.where(keep, pltpu.roll(a, d, 1), 1.0)
            b_sh = jnp.where(keep, pltpu.roll(b, d, 1), 0.0)
            b = a * b_sh + b
            a = a * a_sh
            d *= 2
        a = a.reshape(LRU_ROWS, lru_width)
        b = b.reshape(LRU_ROWS, lru_width)
        yield
        groups = []
        for r0 in range(0, LRU_ROWS, SUBLANES):
            hg = b[r0:r0 + SUBLANES] + a[r0:r0 + SUBLANES] * carry
            carry = hg[SUBLANES - 1:SUBLANES]
            groups.append(hg)
        gate = gate_ref[b0:b0 + LRU_ROWS, :].astype(_F32)
        lru_ref[b0:b0 + LRU_ROWS, :] = (jnp.concatenate(groups, axis=0)
                                        * _gelu_tanh(gate)).astype(lru_ref.dtype)
        yield
    hcarry[...] = carry


def _mlp_kernel(x_ref, gdn_ref, lru_ref, wog_ref, wol_ref, nw_ref, w1_ref, w2_ref, fnw_ref,
                o_ref, *, ff_chunk):
    d_ff = w1_ref.shape[1]
    x1 = (x_ref[...]
          + jnp.dot(gdn_ref[...], wog_ref[...], preferred_element_type=_F32)
          + jnp.dot(lru_ref[...], wol_ref[...], preferred_element_type=_F32))
    mb = _rmsnorm(x1, nw_ref[...]).astype(_BF16)
    ff = None
    for c0 in range(0, d_ff, ff_chunk):
        u = jnp.maximum(
            jnp.dot(mb, w1_ref[:, c0:c0 + ff_chunk], preferred_element_type=_F32), 0.0)
        t = jnp.dot((u * u).astype(_BF16), w2_ref[c0:c0 + ff_chunk, :],
                    preferred_element_type=_F32)
        ff = t if ff is None else ff + t
    o_ref[...] = _rmsnorm(x1 + ff, fnw_ref[...])


def _mlp(x, gdn, lru, wog, wol, nw, w1, w2, fnw, *, tm, ff_chunk):
    n_tok, d_model = x.shape
    tile = lambda w: pl.BlockSpec((tm, w), lambda i: (i, 0))
    consts = (wog, wol, nw, w1, w2, fnw)
    return pl.pallas_call(
        functools.partial(_mlp_kernel, ff_chunk=ff_chunk),
        out_shape=jax.ShapeDtypeStruct((n_tok, d_model), _F32),
        grid=(n_tok // tm,),
        in_specs=[tile(d_model), tile(gdn.shape[1]), tile(lru.shape[1])]
                 + [_const_spec(c.shape) for c in consts],
        out_specs=tile(d_model),
        compiler_params=pltpu.CompilerParams(
            dimension_semantics=("parallel",),
            vmem_limit_bytes=VMEM_LIMIT_BYTES),
        name="outproj_mlp",
    )(x, gdn, lru, *consts)


def _block_diag(w):
    g, i, j = w.shape
    eye = jnp.eye(g, dtype=w.dtype)
    return (eye[:, None, :, None] * w[:, :, None, :]).reshape(g * i, g * j)


def _layer(x, norm_mix_w, w_in, gdn_conv_w, gdn_A_log, gdn_dt_bias, gdn_norm_w,
           lru_conv_w, lru_conv_b, lru_gate_a_w, lru_gate_a_b, lru_gate_x_w, lru_gate_x_b,
           lru_a_param, w_out, norm_mlp_w, w_ff1, w_ff2, final_norm_w, *, seq_tile, tok_tile,
           ff_chunk):
    bsz, seq, d_model = x.shape
    n_heads = gdn_A_log.shape[0]
    gdn_width = n_heads * HEAD_DIM
    lru_width = lru_conv_b.shape[0]
    assert 2 * n_heads <= LANES
    assert seq % seq_tile == 0 and seq_tile % GDN_CHUNK == 0
    assert seq_tile % LRU_ROWS == 0
    assert gdn_conv_w.shape[0] == lru_conv_w.shape[0] == CONV_WIDTH <= CONV_HALO + 1
    assert (bsz * seq) % tok_tile == 0 and w_ff1.shape[1] % ff_chunk == 0
    assert MXU_TILE % lru_gate_a_w.shape[1] == 0 and lru_width % MXU_TILE == 0

    o_qkv, o_z = 0, 3 * gdn_width
    o_b = o_z + gdn_width
    o_a = o_b + n_heads
    o_lx = o_a + n_heads
    o_lg = o_lx + lru_width
    wconv = jnp.concatenate([w_in[:, o_qkv:o_z], w_in[:, o_lx:o_lg]], axis=1).astype(_BF16)
    wgate = jnp.concatenate([w_in[:, o_z:o_b], w_in[:, o_lg:o_lg + lru_width]], axis=1).astype(_BF16)
    wba = jnp.pad(w_in[:, o_b:o_lx], ((0, 0), (0, LANES - 2 * n_heads))).astype(_BF16)
    cw = jnp.concatenate([gdn_conv_w, lru_conv_w], axis=1).astype(_F32)
    pad_heads = lambda p: jnp.pad(p.astype(_F32), (n_heads, LANES - 2 * n_heads)).reshape(1, LANES)
    wlru = jnp.concatenate([_block_diag(lru_gate_a_w), _block_diag(lru_gate_x_w)], axis=1).astype(_BF16)
    blru = jnp.concatenate([lru_gate_a_b.reshape(1, -1), lru_gate_x_b.reshape(1, -1)], axis=1)

    n_tok = bsz * seq
    x2d = x.reshape(n_tok, d_model)
    gdn, lru = _mixer(
        x2d, norm_mix_w.reshape(1, -1), wconv, wgate, wba, cw, lru_conv_b.reshape(1, -1),
        pad_heads(gdn_A_log), pad_heads(gdn_dt_bias), gdn_norm_w.reshape(1, -1),
        wlru, blru, lru_a_param.reshape(1, -1), ts=seq_tile, seq=seq)

    out = _mlp(x2d, gdn, lru,
               w_out[:gdn_width].astype(_BF16), w_out[gdn_width:].astype(_BF16),
               norm_mlp_w.reshape(1, -1), w_ff1.astype(_BF16), w_ff2.astype(_BF16),
               final_norm_w.reshape(1, -1), tm=tok_tile, ff_chunk=ff_chunk)
    return out.reshape(bsz, seq, d_model)


def kernel(x, norm_mix_w, w_in, gdn_conv_w, gdn_A_log, gdn_dt_bias, gdn_norm_w, lru_conv_w, lru_conv_b, lru_gate_a_w, lru_gate_a_b, lru_gate_x_w, lru_gate_x_b, lru_a_param, w_out, norm_mlp_w, w_ff1, w_ff2, final_norm_w):
    assert norm_mix_w.shape[0] == 1, "single-layer stack"
    return _layer(x, norm_mix_w[0], w_in[0], gdn_conv_w[0], gdn_A_log[0], gdn_dt_bias[0],
                  gdn_norm_w[0], lru_conv_w[0], lru_conv_b[0], lru_gate_a_w[0], lru_gate_a_b[0],
                  lru_gate_x_w[0], lru_gate_x_b[0], lru_a_param[0], w_out[0], norm_mlp_w[0],
                  w_ff1[0], w_ff2[0], final_norm_w, seq_tile=256, tok_tile=1024, ff_chunk=2048)
```

```python
import functools
import math

import jax
import jax.numpy as jnp
from jax import lax
from jax.experimental import pallas as pl
from jax.experimental.pallas import tpu as pltpu

EPS = 1e-6
HEAD_DIM = 128
LANES = 128
SUBLANES = 8
GDN_CHUNK = 64
INV_BASE = 16
LRU_C = 8.0
CONV_WIDTH = 4
CONV_HALO = 8
PROJ_COLS = 256
GDN_TILE = 256
LRU_ROWS = 512
F32_TINY = 1.1754944e-38
MXU_TILE = 256
VMEM_LIMIT_BYTES = 56 * 1024 * 1024

_BF16 = jnp.bfloat16
_F32 = jnp.float32
_NT_DIMS = (((1,), (1,)), ((), ()))


def _dot(a, b):
    return jnp.dot(a.astype(_BF16), b.astype(_BF16), preferred_element_type=_F32)


def _dot_nt(a, b):
    return lax.dot_general(a.astype(_BF16), b.astype(_BF16), _NT_DIMS,
                           preferred_element_type=_F32)


def _sigmoid(x):
    return 1.0 / (1.0 + jnp.exp(-x))


def _silu(x):
    hx = 0.5 * x
    return hx * jnp.tanh(hx) + hx


def _softplus(x):
    return jnp.maximum(x, 0.0) + jnp.log1p(jnp.exp(-jnp.abs(x)))


def _gelu_tanh(x):
    c = math.sqrt(2.0 / math.pi)
    hx = 0.5 * x
    return hx * jnp.tanh(x * (c + (c * 0.044715) * (x * x))) + hx


def _rmsnorm(x, w):
    return x * lax.rsqrt(jnp.mean(x * x, axis=-1, keepdims=True) + EPS) * w


def _const_spec(shape):
    nd = len(shape)
    return pl.BlockSpec(shape, lambda *_: (0,) * nd, pipeline_mode=pl.Buffered(1))


def _interleave(*stages):
    live = list(stages)
    while live:
        for entry in list(live):
            stage, pieces = entry
            for _ in range(pieces):
                try:
                    next(stage)
                except StopIteration:
                    live.remove(entry)
                    break


def _causal_conv(pbuf, cw_ref, ts):
    pe = pbuf[...]
    pd = pltpu.roll(pe, 1, 0)
    near = cw_ref[3:4, :] * pe + cw_ref[2:3, :] * pd
    far = cw_ref[1:2, :] * pe + cw_ref[0:1, :] * pd
    y = near + pltpu.roll(far, 2, 0)
    return y[CONV_HALO:CONV_HALO + ts, :]


def _pack_chunks(m, lane_chunk):
    n = m.shape[0] // GDN_CHUNK
    out = m[(n - 1) * GDN_CHUNK:, :]
    for c in range(n - 2, -1, -1):
        out = jnp.where(lane_chunk == c, m[c * GDN_CHUNK:(c + 1) * GDN_CHUNK, :], out)
    return out


def _block_diag_of(p, bd_mask):
    pb = p.astype(_BF16)
    return jnp.concatenate([pb] * (bd_mask.shape[0] // GDN_CHUNK), axis=0) * bd_mask


def _packed_dot(x, p_bd):
    return jnp.dot(x.astype(_BF16), p_bd, preferred_element_type=_F32)


def _unit_lower_inverse(a_heads, r, j, bd_mask):
    c = GDN_CHUNK
    eye = (r == j).astype(_F32)
    base = (r // INV_BASE) == (j // INV_BASE)
    d = [jnp.where(base, a, 0.0) for a in a_heads]
    x = [eye - dh for dh in d]
    p = [_packed_dot(dh, _block_diag_of(dh, bd_mask)) for dh in d]
    span = 4
    while span < INV_BASE:
        both = [_packed_dot(jnp.concatenate([xh, ph], axis=0), _block_diag_of(ph, bd_mask))
                for xh, ph in zip(x, p)]
        x = [xh + bh[:c] for xh, bh in zip(x, both)]
        p = [bh[c:] for bh in both]
        span *= 2
    x = [xh + _packed_dot(xh, _block_diag_of(ph, bd_mask)) for xh, ph in zip(x, p)]
    size = INV_BASE
    while size < GDN_CHUNK:
        off = ((r // (2 * size)) == (j // (2 * size))) & ((r // size) != (j // size))
        xe = [_packed_dot(xh, _block_diag_of(jnp.where(off, a, 0.0), bd_mask))
              for xh, a in zip(x, a_heads)]
        x = [xh - _packed_dot(xeh, _block_diag_of(xh, bd_mask)) for xh, xeh in zip(x, xe)]
        size *= 2
    return x


def _gdn_chunk_operands(q_all, k_all, v_all, gb, lhs_s, n_s, o2_s, gam_s, chunk0):
    n_heads = len(q_all)
    heads = range(n_heads)
    ts = gb.shape[0]
    n_chunks = ts // GDN_CHUNK
    row = lax.broadcasted_iota(jnp.int32, (ts, ts), 0)
    col = lax.broadcasted_iota(jnp.int32, (ts, ts), 1)
    same_chunk = (row // GDN_CHUNK) == (col // GDN_CHUNK)
    bd_mask = jnp.where(same_chunk, 1.0, 0.0).astype(_BF16)
    r = lax.broadcasted_iota(jnp.int32, (GDN_CHUNK, ts), 0)
    lane = lax.broadcasted_iota(jnp.int32, (GDN_CHUNK, ts), 1)
    lane_chunk = lane // GDN_CHUNK
    j = lane % GDN_CHUNK
    causal = r >= j
    strict = r > j

    ones_tri = jnp.where(same_chunk & (row >= col), 1.0, 0.0).astype(_BF16)
    gb_hi = gb.astype(_BF16)
    gb_r1 = gb - gb_hi.astype(_F32)
    gb_mid = gb_r1.astype(_BF16)
    gb_lo = (gb_r1 - gb_mid.astype(_F32)).astype(_BF16)
    gcum = (jnp.dot(ones_tri, gb_hi, preferred_element_type=_F32)
            + jnp.dot(ones_tri, gb_mid, preferred_element_type=_F32)
            + jnp.dot(ones_tri, gb_lo, preferred_element_type=_F32))
    gcum_t = gcum.T

    gcol_all, a_all, rhs_all, qk_all, qg_all = [], [], [], [], []
    for hd in heads:
        beta = gb[:, hd:hd + 1]
        gcol = gcum[:, n_heads + hd:n_heads + hd + 1]
        grow = gcum_t[n_heads + hd:n_heads + hd + 1, :]
        gpk = _pack_chunks(jnp.broadcast_to(gcol, (ts, ts)), lane_chunk)
        decay = jnp.where(causal, jnp.exp(jnp.where(causal, gpk - grow, 0.0)), 0.0)
        q, k, v = q_all[hd], k_all[hd], v_all[hd]
        egc = jnp.exp(gcol)
        kb = k * beta
        gcol_all.append(gcol)
        a_all.append(jnp.where(strict, _pack_chunks(_dot_nt(kb, k), lane_chunk) * decay, 0.0))
        rhs_all.append(jnp.concatenate([v * beta, kb * egc], axis=1))
        qk_all.append(_pack_chunks(_dot_nt(q, k), lane_chunk) * decay)
        qg_all.append(q * egc)
    t_all = _unit_lower_inverse(a_all, r, j, bd_mask)
    sol_all = [jnp.dot(_block_diag_of(t, bd_mask), rhs.astype(_BF16), preferred_element_type=_F32)
               for t, rhs in zip(t_all, rhs_all)]

    for c in range(n_chunks):
        r0 = c * GDN_CHUNK
        r1 = r0 + GDN_CHUNK
        for hd in heads:
            idx = hd * (lhs_s.shape[0] // n_heads) + chunk0 + c
            sol = sol_all[hd][r0:r1]
            g_last = gcol_all[hd][r1 - 1:r1, :]
            k_dec = k_all[hd][r0:r1] * jnp.exp(g_last - gcol_all[hd][r0:r1])
            kd_sol = _dot(k_dec.T, sol)
            qk_sol = _dot(qk_all[hd][:, r0:r1], sol)
            lhs_s[idx] = jnp.concatenate(
                [kd_sol[:, HEAD_DIM:], qg_all[hd][r0:r1] - qk_sol[:, HEAD_DIM:]],
                axis=0).astype(lhs_s.dtype)
            n_s[idx] = kd_sol[:, :HEAD_DIM]
            o2_s[idx] = qk_sol[:, :HEAD_DIM]
            gam_s[idx] = jnp.broadcast_to(jnp.exp(g_last), gam_s.shape[1:])


def _gdn_recurrence_stage(lhs_s, n_s, o2_s, gam_s, z_ref, gnw_ref, state, gdn_ref):
    n_heads = state.shape[0]
    heads = range(n_heads)
    n_chunks = lhs_s.shape[0] // n_heads
    s_all = [state[hd] for hd in heads]
    outs = [[] for _ in heads]
    for c in range(n_chunks):
        for hd in heads:
            idx = hd * n_chunks + c
            s = s_all[hd]
            r = jnp.dot(lhs_s[idx], s.astype(_BF16), preferred_element_type=_F32)
            outs[hd].append(r[HEAD_DIM:] + o2_s[idx])
            s_all[hd] = s * gam_s[idx][0:1, :] - r[:HEAD_DIM] + n_s[idx]
        yield
    for hd in heads:
        lo = hd * HEAD_DIM
        state[hd] = s_all[hd]
        z = z_ref[:, lo:lo + HEAD_DIM]
        o = _rmsnorm(jnp.concatenate(outs[hd], axis=0), gnw_ref[...]) * _silu(z)
        gdn_ref[:, lo:lo + HEAD_DIM] = o.astype(gdn_ref.dtype)
        yield


def _project_stage(x_ref, nw_ref, wconv_ref, wgate_ref, wba_ref, pbuf, out):
    ts = x_ref.shape[0]
    hb = _rmsnorm(x_ref[...], nw_ref[...]).astype(_BF16)
    yield
    for c0 in range(0, wconv_ref.shape[1], PROJ_COLS):
        pbuf[CONV_HALO:CONV_HALO + ts, c0:c0 + PROJ_COLS] = jnp.dot(
            hb, wconv_ref[:, c0:c0 + PROJ_COLS], preferred_element_type=_F32)
        yield
    parts = []
    for c0 in range(0, wgate_ref.shape[1], PROJ_COLS):
        parts.append(jnp.dot(hb, wgate_ref[:, c0:c0 + PROJ_COLS], preferred_element_type=_F32))
        yield
    out["pg"] = jnp.concatenate(parts, axis=1)
    out["pba"] = jnp.dot(hb, wba_ref[...], preferred_element_type=_F32)


def _mixer_kernel(x_ref, nw_ref, wconv_ref, wgate_ref, wba_ref, cw_ref, lcb_ref,
                  alog_ref, dtb_ref, gnw_ref, wlru_ref, blru_ref, ap_ref,
                  gdn_ref, lru_ref,
                  pbuf, z_s, lhs_s, n_s, o2_s, gam_s, state, xr_s, gate_s, hcarry, *,
                  tiles_per_seq):
    j = pl.program_id(0)
    ts = x_ref.shape[0]
    gdn_width = gdn_ref.shape[1]
    n_heads = gdn_width // HEAD_DIM

    @pl.when(j == 0)
    def _():
        for ref in (z_s, lhs_s, n_s, o2_s, gam_s, xr_s, gate_s):
            ref[...] = jnp.zeros_like(ref)

    @pl.when(j % tiles_per_seq == 0)
    def _():
        pbuf[0:CONV_HALO, :] = jnp.zeros((CONV_HALO, pbuf.shape[1]), _F32)

    @pl.when((j == 0) | (j % tiles_per_seq == 1 % tiles_per_seq))
    def _():
        state[...] = jnp.zeros_like(state)
        hcarry[...] = jnp.zeros_like(hcarry)

    proj = {}
    _interleave(
        (_project_stage(x_ref, nw_ref, wconv_ref, wgate_ref, wba_ref, pbuf, proj), 2),
        (_lru_stage(xr_s, gate_s, wlru_ref, blru_ref, ap_ref, hcarry, lru_ref), 1),
        (_gdn_recurrence_stage(lhs_s, n_s, o2_s, gam_s, z_s, gnw_ref, state, gdn_ref), 1))

    y = _causal_conv(pbuf, cw_ref, ts)
    pbuf[0:CONV_HALO, :] = pbuf[ts:ts + CONV_HALO, :]

    pba = proj["pba"]
    lane = lax.broadcasted_iota(jnp.int32, pba.shape, 1)
    gb = jnp.where(lane < n_heads, _sigmoid(pba),
                   -jnp.exp(alog_ref[...]) * _softplus(pba + dtb_ref[...]))

    qkv = y[:, :3 * gdn_width]
    qkv = _silu(qkv)
    scale = HEAD_DIM ** -0.5
    q_all, k_all, v_all = [], [], []
    for hd in range(n_heads):
        lo = hd * HEAD_DIM
        qh = qkv[:, lo:lo + HEAD_DIM]
        kh = qkv[:, gdn_width + lo:gdn_width + lo + HEAD_DIM]
        q_all.append(qh * (lax.rsqrt(jnp.sum(qh * qh, axis=-1, keepdims=True) + EPS) * scale))
        k_all.append(kh * lax.rsqrt(jnp.sum(kh * kh, axis=-1, keepdims=True) + EPS))
        v_all.append(qkv[:, 2 * gdn_width + lo:2 * gdn_width + lo + HEAD_DIM])

    for r0 in range(0, ts, GDN_TILE):
        rows = slice(r0, r0 + GDN_TILE)
        _gdn_chunk_operands([q[rows] for q in q_all], [k[rows] for k in k_all],
                            [v[rows] for v in v_all], gb[rows], lhs_s, n_s, o2_s, gam_s,
                            chunk0=r0 // GDN_CHUNK)
    pg = proj["pg"]
    z_s[...] = pg[:, :gdn_width]

    xr_s[...] = (y[:, 3 * gdn_width:] + lcb_ref[...]).astype(xr_s.dtype)
    gate_s[...] = pg[:, gdn_width:].astype(gate_s.dtype)


def _mixer(x, nw, wconv, wgate, wba, cw, lcb, alog, dtb, gnw, wlru, blru, ap, *, ts, seq):
    n_tok, d_model = x.shape
    lru_width = lcb.shape[1]
    gdn_width = wgate.shape[1] - lru_width
    n_heads = gdn_width // HEAD_DIM
    n_tiles = n_tok // ts
    n_ops = n_heads * (ts // GDN_CHUNK)
    consts = (nw, wconv, wgate, wba, cw, lcb, alog, dtb, gnw, wlru, blru, ap)
    front_tile = lambda w: pl.BlockSpec((ts, w), lambda j: (jnp.minimum(j, n_tiles - 1), 0))
    back_tile = lambda w: pl.BlockSpec((ts, w), lambda j: (jnp.maximum(j - 1, 0), 0))
    return pl.pallas_call(
        functools.partial(_mixer_kernel, tiles_per_seq=seq // ts),
        out_shape=(jax.ShapeDtypeStruct((n_tok, gdn_width), _BF16),
                   jax.ShapeDtypeStruct((n_tok, lru_width), _BF16)),
        grid=(n_tiles + 1,),
        in_specs=[front_tile(d_model)] + [_const_spec(c.shape) for c in consts],
        out_specs=(back_tile(gdn_width), back_tile(lru_width)),
        scratch_shapes=[pltpu.VMEM((ts + CONV_HALO, wconv.shape[1]), _F32),
                        pltpu.VMEM((ts, gdn_width), _F32),
                        pltpu.VMEM((n_ops, HEAD_DIM + GDN_CHUNK, HEAD_DIM), _BF16),
                        pltpu.VMEM((n_ops, HEAD_DIM, HEAD_DIM), _F32),
                        pltpu.VMEM((n_ops, GDN_CHUNK, HEAD_DIM), _F32),
                        pltpu.VMEM((n_ops, SUBLANES, LANES), _F32),
                        pltpu.VMEM((n_heads, HEAD_DIM, HEAD_DIM), _F32),
                        pltpu.VMEM((ts, lru_width), _BF16),
                        pltpu.VMEM((ts, lru_width), _BF16),
                        pltpu.VMEM((1, lru_width), _F32)],
        compiler_params=pltpu.CompilerParams(
            dimension_semantics=("arbitrary",),
            vmem_limit_bytes=VMEM_LIMIT_BYTES),
        name="mixer",
    )(x, *consts)


def _block_diag_dot(xb, w_ref, col0):
    k = xb.shape[1]
    return jnp.concatenate(
        [jnp.dot(xb[:, t:t + MXU_TILE], w_ref[t:t + MXU_TILE, col0 + t:col0 + t + MXU_TILE],
                 preferred_element_type=_F32) for t in range(0, k, MXU_TILE)], axis=1)


def _lru_stage(xr_ref, gate_ref, wlru_ref, blru_ref, ap_ref, hcarry, lru_ref):
    ts, lru_width = xr_ref.shape
    nsp = _softplus(-ap_ref[...])
    row = lax.broadcasted_iota(jnp.int32, (LRU_ROWS // SUBLANES, SUBLANES, lru_width), 1)
    carry = hcarry[...]
    for b0 in range(0, ts, LRU_ROWS):
        xb = xr_ref[b0:b0 + LRU_ROWS, :]
        xr = xb.astype(_F32)
        r = _sigmoid(_block_diag_dot(xb, wlru_ref, 0) + blru_ref[:, :lru_width])
        log_a = -LRU_C * r * nsp
        a = jnp.exp(log_a)
        yield
        i = _sigmoid(_block_diag_dot(xb, wlru_ref, lru_width) + blru_ref[:, lru_width:])
        v = jnp.maximum(1.0 - a * a, 0.0)
        b = v * lax.rsqrt(jnp.maximum(v, F32_TINY)) * (i * xr)
        yield
        a = a.reshape(LRU_ROWS // SUBLANES, SUBLANES, lru_width)
        b = b.reshape(LRU_ROWS // SUBLANES, SUBLANES, lru_width)
        d = 1
        while d < SUBLANES:
            keep = row >= d
            a_sh = jnp.where(keep, pltpu.roll(a, d, 1), 1.0)
            b_sh = jnp.where(keep, pltpu.roll(b, d, 1), 0.0)
            b = a * b_sh + b
            a = a * a_sh
            d *= 2
        a = a.reshape(LRU_ROWS, lru_width)
        b = b.reshape(LRU_ROWS, lru_width)
        yield
        groups = []
        for r0 in range(0, LRU_ROWS, SUBLANES):
            hg = b[r0:r0 + SUBLANES] + a[r0:r0 + SUBLANES] * carry
            carry = hg[SUBLANES - 1:SUBLANES]
            groups.append(hg)
        gate = gate_ref[b0:b0 + LRU_ROWS, :].astype(_F32)
        lru_ref[b0:b0 + LRU_ROWS, :] = (jnp.concatenate(groups, axis=0)
                                        * _gelu_tanh(gate)).astype(lru_ref.dtype)
        yield
    hcarry[...] = carry


def _mlp_kernel(x_ref, gdn_ref, lru_ref, wog_ref, wol_ref, nw_ref, w1_ref, w2_ref, fnw_ref,
                o_ref, *, ff_chunk):
    d_ff = w1_ref.shape[1]
    x1 = (x_ref[...]
          + jnp.dot(gdn_ref[...], wog_ref[...], preferred_element_type=_F32)
          + jnp.dot(lru_ref[...], wol_ref[...], preferred_element_type=_F32))
    mb = _rmsnorm(x1, nw_ref[...]).astype(_BF16)
    ff = None
    for c0 in range(0, d_ff, ff_chunk):
        u = jnp.maximum(
            jnp.dot(mb, w1_ref[:, c0:c0 + ff_chunk], preferred_element_type=_F32), 0.0)
        t = jnp.dot((u * u).astype(_BF16), w2_ref[c0:c0 + ff_chunk, :],
                    preferred_element_type=_F32)
        ff = t if ff is None else ff + t
    o_ref[...] = _rmsnorm(x1 + ff, fnw_ref[...])


def _mlp(x, gdn, lru, wog, wol, nw, w1, w2, fnw, *, tm, ff_chunk):
    n_tok, d_model = x.shape
    tile = lambda w: pl.BlockSpec((tm, w), lambda i: (i, 0))
    consts = (wog, wol, nw, w1, w2, fnw)
    return pl.pallas_call(
        functools.partial(_mlp_kernel, ff_chunk=ff_chunk),
        out_shape=jax.ShapeDtypeStruct((n_tok, d_model), _F32),
        grid=(n_tok // tm,),
        in_specs=[tile(d_model), tile(gdn.shape[1]), tile(lru.shape[1])]
                 + [_const_spec(c.shape) for c in consts],
        out_specs=tile(d_model),
        compiler_params=pltpu.CompilerParams(
            dimension_semantics=("parallel",),
            vmem_limit_bytes=VMEM_LIMIT_BYTES),
        name="outproj_mlp",
    )(x, gdn, lru, *consts)


def _block_diag(w):
    g, i, j = w.shape
    eye = jnp.eye(g, dtype=w.dtype)
    return (eye[:, None, :, None] * w[:, :, None, :]).reshape(g * i, g * j)


def _layer(x, norm_mix_w, w_in, gdn_conv_w, gdn_A_log, gdn_dt_bias, gdn_norm_w,
           lru_conv_w, lru_conv_b, lru_gate_a_w, lru_gate_a_b, lru_gate_x_w, lru_gate_x_b,
           lru_a_param, w_out, norm_mlp_w, w_ff1, w_ff2, final_norm_w, *, seq_tile, tok_tile,
           ff_chunk):
    bsz, seq, d_model = x.shape
    n_heads = gdn_A_log.shape[0]
    gdn_width = n_heads * HEAD_DIM
    lru_width = lru_conv_b.shape[0]
    assert 2 * n_heads <= LANES
    assert seq % seq_tile == 0 and seq_tile % GDN_TILE == 0 and GDN_TILE % GDN_CHUNK == 0
    assert seq_tile % LRU_ROWS == 0
    assert (bsz * seq) % tok_tile == 0 and w_ff1.shape[1] % ff_chunk == 0
    assert MXU_TILE % lru_gate_a_w.shape[1] == 0 and lru_width % MXU_TILE == 0

    o_qkv, o_z = 0, 3 * gdn_width
    o_b = o_z + gdn_width
    o_a = o_b + n_heads
    o_lx = o_a + n_heads
    o_lg = o_lx + lru_width
    wconv = jnp.concatenate([w_in[:, o_qkv:o_z], w_in[:, o_lx:o_lg]], axis=1).astype(_BF16)
    wgate = jnp.concatenate([w_in[:, o_z:o_b], w_in[:, o_lg:o_lg + lru_width]], axis=1).astype(_BF16)
    wba = jnp.pad(w_in[:, o_b:o_lx], ((0, 0), (0, LANES - 2 * n_heads))).astype(_BF16)
    cw = jnp.concatenate([gdn_conv_w, lru_conv_w], axis=1).astype(_F32)
    pad_heads = lambda p: jnp.pad(p.astype(_F32), (n_heads, LANES - 2 * n_heads)).reshape(1, LANES)
    wlru = jnp.concatenate([_block_diag(lru_gate_a_w), _block_diag(lru_gate_x_w)], axis=1).astype(_BF16)
    blru = jnp.concatenate([lru_gate_a_b.reshape(1, -1), lru_gate_x_b.reshape(1, -1)], axis=1)

    n_tok = bsz * seq
    x2d = x.reshape(n_tok, d_model)
    gdn, lru = _mixer(
        x2d, norm_mix_w.reshape(1, -1), wconv, wgate, wba, cw, lru_conv_b.reshape(1, -1),
        pad_heads(gdn_A_log), pad_heads(gdn_dt_bias), gdn_norm_w.reshape(1, -1),
        wlru, blru, lru_a_param.reshape(1, -1), ts=seq_tile, seq=seq)

    out = _mlp(x2d, gdn, lru,
               w_out[:gdn_width].astype(_BF16), w_out[gdn_width:].astype(_BF16),
               norm_mlp_w.reshape(1, -1), w_ff1.astype(_BF16), w_ff2.astype(_BF16),
               final_norm_w.reshape(1, -1), tm=tok_tile, ff_chunk=ff_chunk)
    return out.reshape(bsz, seq, d_model)


def kernel(x, norm_mix_w, w_in, gdn_conv_w, gdn_A_log, gdn_dt_bias, gdn_norm_w, lru_conv_w, lru_conv_b, lru_gate_a_w, lru_gate_a_b, lru_gate_x_w, lru_gate_x_b, lru_a_param, w_out, norm_mlp_w, w_ff1, w_ff2, final_norm_w):
    assert norm_mix_w.shape[0] == 1, "single-layer stack"
    return _layer(x, norm_mix_w[0], w_in[0], gdn_conv_w[0], gdn_A_log[0], gdn_dt_bias[0],
                  gdn_norm_w[0], lru_conv_w[0], lru_conv_b[0], lru_gate_a_w[0], lru_gate_a_b[0],
                  lru_gate_x_w[0], lru_gate_x_b[0], lru_a_param[0], w_out[0], norm_mlp_w[0],
                  w_ff1[0], w_ff2[0], final_norm_w, seq_tile=512, tok_tile=1024, ff_chunk=2048)
```

```python
import functools
import math
from typing import NamedTuple

import jax
import jax.numpy as jnp
from jax import lax
from jax.experimental import pallas as pl
from jax.experimental.pallas import tpu as pltpu

EPS = 1e-6
HEAD_DIM = 128
LANES = 128
SUBLANES = 8
GDN_CHUNK = 64
INV_BASE = 16
LRU_C = 8.0
CONV_WIDTH = 4
CONV_HALO = 8
PROJ_COLS = 256
GDN_TILE = 256
LRU_ROWS = 512
F32_TINY = 1.1754944e-38
MXU_TILE = 256
VMEM_LIMIT_BYTES = 56 * 1024 * 1024

_BF16 = jnp.bfloat16
_F32 = jnp.float32
_NT_DIMS = (((1,), (1,)), ((), ()))


def _dot(a, b):
    return jnp.dot(a.astype(_BF16), b.astype(_BF16), preferred_element_type=_F32)


def _dot_nt(a, b):
    return lax.dot_general(a.astype(_BF16), b.astype(_BF16), _NT_DIMS,
                           preferred_element_type=_F32)


def _sigmoid(x):
    return 1.0 / (1.0 + jnp.exp(-x))


def _silu(x):
    hx = 0.5 * x
    return hx * jnp.tanh(hx) + hx


def _softplus(x):
    return jnp.maximum(x, 0.0) + jnp.log1p(jnp.exp(-jnp.abs(x)))


def _gelu_tanh(x):
    c = math.sqrt(2.0 / math.pi)
    hx = 0.5 * x
    return hx * jnp.tanh(x * (c + (c * 0.044715) * (x * x))) + hx


def _rmsnorm(x, w):
    return x * lax.rsqrt(jnp.mean(x * x, axis=-1, keepdims=True) + EPS) * w


def _const_spec(shape):
    nd = len(shape)
    return pl.BlockSpec(shape, lambda *_: (0,) * nd, pipeline_mode=pl.Buffered(1))


def _interleave(*stages):
    live = list(stages)
    while live:
        for entry in list(live):
            stage, pieces = entry
            for _ in range(pieces):
                try:
                    next(stage)
                except StopIteration:
                    live.remove(entry)
                    break


def _causal_conv(pbuf, cw_ref, ts):
    pe = pbuf[...]
    pd = pltpu.roll(pe, 1, 0)
    near = cw_ref[3:4, :] * pe + cw_ref[2:3, :] * pd
    far = cw_ref[1:2, :] * pe + cw_ref[0:1, :] * pd
    y = near + pltpu.roll(far, 2, 0)
    return y[CONV_HALO:CONV_HALO + ts, :]


def _pack_chunks(m, lane_chunk):
    n = m.shape[0] // GDN_CHUNK
    out = m[(n - 1) * GDN_CHUNK:, :]
    for c in range(n - 2, -1, -1):
        out = jnp.where(lane_chunk == c, m[c * GDN_CHUNK:(c + 1) * GDN_CHUNK, :], out)
    return out


def _block_diag_of(p, bd_mask):
    pb = p.astype(_BF16)
    return jnp.concatenate([pb] * (bd_mask.shape[0] // GDN_CHUNK), axis=0) * bd_mask


def _packed_dot(x, p_bd):
    return jnp.dot(x.astype(_BF16), p_bd, preferred_element_type=_F32)


def _unit_lower_inverse(a_heads, r, j, bd_mask):
    c = GDN_CHUNK
    eye = (r == j).astype(_F32)
    base = (r // INV_BASE) == (j // INV_BASE)
    d = [jnp.where(base, a, 0.0) for a in a_heads]
    x = [eye - dh for dh in d]
    p = [_packed_dot(dh, _block_diag_of(dh, bd_mask)) for dh in d]
    span = 4
    while span < INV_BASE:
        both = [_packed_dot(jnp.concatenate([xh, ph], axis=0), _block_diag_of(ph, bd_mask))
                for xh, ph in zip(x, p)]
        x = [xh + bh[:c] for xh, bh in zip(x, both)]
        p = [bh[c:] for bh in both]
        span *= 2
    x = [xh + _packed_dot(xh, _block_diag_of(ph, bd_mask)) for xh, ph in zip(x, p)]
    size = INV_BASE
    while size < GDN_CHUNK:
        off = ((r // (2 * size)) == (j // (2 * size))) & ((r // size) != (j // size))
        xe = [_packed_dot(xh, _block_diag_of(jnp.where(off, a, 0.0), bd_mask))
              for xh, a in zip(x, a_heads)]
        x = [xh - _packed_dot(xeh, _block_diag_of(xh, bd_mask)) for xh, xeh in zip(x, xe)]
        size *= 2
    return x


class _GdnProblem(NamedTuple):
    q: jax.Array
    k: jax.Array
    v: jax.Array
    gb: jax.Array
    head: int
    chunk0: int


def _gdn_chunk_operands(problems, n_heads, lhs_s, n_s, o2_s, gam_s):
    ts = GDN_TILE
    n_chunks = ts // GDN_CHUNK
    chunks_per_head = lhs_s.shape[0] // n_heads
    row = lax.broadcasted_iota(jnp.int32, (ts, ts), 0)
    col = lax.broadcasted_iota(jnp.int32, (ts, ts), 1)
    same_chunk = (row // GDN_CHUNK) == (col // GDN_CHUNK)
    bd_mask = jnp.where(same_chunk, 1.0, 0.0).astype(_BF16)
    r = lax.broadcasted_iota(jnp.int32, (GDN_CHUNK, ts), 0)
    lane = lax.broadcasted_iota(jnp.int32, (GDN_CHUNK, ts), 1)
    lane_chunk = lane // GDN_CHUNK
    j = lane % GDN_CHUNK
    causal = r >= j
    strict = r > j

    ones_tri = jnp.where(same_chunk & (row >= col), 1.0, 0.0).astype(_BF16)
    cumsums = {}
    for p in problems:
        if p.chunk0 not in cumsums:
            gb_hi = p.gb.astype(_BF16)
            gb_r1 = p.gb - gb_hi.astype(_F32)
            gb_mid = gb_r1.astype(_BF16)
            gb_lo = (gb_r1 - gb_mid.astype(_F32)).astype(_BF16)
            gcum = (jnp.dot(ones_tri, gb_hi, preferred_element_type=_F32)
                    + jnp.dot(ones_tri, gb_mid, preferred_element_type=_F32)
                    + jnp.dot(ones_tri, gb_lo, preferred_element_type=_F32))
            cumsums[p.chunk0] = (gcum, gcum.T)

    gcol_all, a_all, rhs_all, qk_all, qg_all = [], [], [], [], []
    for p in problems:
        gcum, gcum_t = cumsums[p.chunk0]
        beta = p.gb[:, p.head:p.head + 1]
        gcol = gcum[:, n_heads + p.head:n_heads + p.head + 1]
        grow = gcum_t[n_heads + p.head:n_heads + p.head + 1, :]
        gpk = _pack_chunks(jnp.broadcast_to(gcol, (ts, ts)), lane_chunk)
        decay = jnp.where(causal, jnp.exp(jnp.where(causal, gpk - grow, 0.0)), 0.0)
        egc = jnp.exp(gcol)
        kb = p.k * beta
        gcol_all.append(gcol)
        a_all.append(jnp.where(strict, _pack_chunks(_dot_nt(kb, p.k), lane_chunk) * decay, 0.0))
        rhs_all.append(jnp.concatenate([p.v * beta, kb * egc], axis=1))
        qk_all.append(_pack_chunks(_dot_nt(p.q, p.k), lane_chunk) * decay)
        qg_all.append(p.q * egc)
    t_all = _unit_lower_inverse(a_all, r, j, bd_mask)
    sol_all = [jnp.dot(_block_diag_of(t, bd_mask), rhs.astype(_BF16), preferred_element_type=_F32)
               for t, rhs in zip(t_all, rhs_all)]

    for c in range(n_chunks):
        r0 = c * GDN_CHUNK
        r1 = r0 + GDN_CHUNK
        for i, p in enumerate(problems):
            idx = p.head * chunks_per_head + p.chunk0 + c
            sol = sol_all[i][r0:r1]
            g_last = gcol_all[i][r1 - 1:r1, :]
            k_dec = p.k[r0:r1] * jnp.exp(g_last - gcol_all[i][r0:r1])
            kd_sol = _dot(k_dec.T, sol)
            qk_sol = _dot(qk_all[i][:, r0:r1], sol)
            lhs_s[idx] = jnp.concatenate(
                [kd_sol[:, HEAD_DIM:], qg_all[i][r0:r1] - qk_sol[:, HEAD_DIM:]],
                axis=0).astype(lhs_s.dtype)
            n_s[idx] = kd_sol[:, :HEAD_DIM]
            o2_s[idx] = qk_sol[:, :HEAD_DIM]
            gam_s[idx] = jnp.broadcast_to(jnp.exp(g_last), gam_s.shape[1:])


def _gdn_recurrence_stage(lhs_s, n_s, o2_s, gam_s, z_ref, gnw_ref, state, gdn_ref):
    n_heads = state.shape[0]
    heads = range(n_heads)
    n_chunks = lhs_s.shape[0] // n_heads
    s_all = [state[hd] for hd in heads]
    outs = [[] for _ in heads]
    for c in range(n_chunks):
        for hd in heads:
            idx = hd * n_chunks + c
            s = s_all[hd]
            r = jnp.dot(lhs_s[idx], s.astype(_BF16), preferred_element_type=_F32)
            outs[hd].append(r[HEAD_DIM:] + o2_s[idx])
            s_all[hd] = s * gam_s[idx][0:1, :] - r[:HEAD_DIM] + n_s[idx]
        yield
    for hd in heads:
        lo = hd * HEAD_DIM
        state[hd] = s_all[hd]
        z = z_ref[:, lo:lo + HEAD_DIM]
        o = _rmsnorm(jnp.concatenate(outs[hd], axis=0), gnw_ref[...]) * _silu(z)
        gdn_ref[:, lo:lo + HEAD_DIM] = o.astype(gdn_ref.dtype)
        yield


def _project_stage(x_ref, nw_ref, wconv_ref, wgate_ref, wba_ref, pbuf, out):
    ts = x_ref.shape[0]
    hb = _rmsnorm(x_ref[...], nw_ref[...]).astype(_BF16)
    yield
    for c0 in range(0, wconv_ref.shape[1], PROJ_COLS):
        pbuf[CONV_HALO:CONV_HALO + ts, c0:c0 + PROJ_COLS] = jnp.dot(
            hb, wconv_ref[:, c0:c0 + PROJ_COLS], preferred_element_type=_F32)
        yield
    parts = []
    for c0 in range(0, wgate_ref.shape[1], PROJ_COLS):
        parts.append(jnp.dot(hb, wgate_ref[:, c0:c0 + PROJ_COLS], preferred_element_type=_F32))
        yield
    out["pg"] = jnp.concatenate(parts, axis=1)
    out["pba"] = jnp.dot(hb, wba_ref[...], preferred_element_type=_F32)


def _mixer_kernel(x_ref, nw_ref, wconv_ref, wgate_ref, wba_ref, cw_ref, lcb_ref,
                  alog_ref, dtb_ref, gnw_ref, wlru_ref, blru_ref, ap_ref,
                  gdn_ref, lru_ref,
                  pbuf, z_s, lhs_s, n_s, o2_s, gam_s, state, xr_s, gate_s, hcarry, *,
                  tiles_per_seq):
    j = pl.program_id(0)
    ts = x_ref.shape[0]
    gdn_width = gdn_ref.shape[1]
    n_heads = gdn_width // HEAD_DIM

    @pl.when(j == 0)
    def _():
        for ref in (z_s, lhs_s, n_s, o2_s, gam_s, xr_s, gate_s):
            ref[...] = jnp.zeros_like(ref)

    @pl.when(j % tiles_per_seq == 0)
    def _():
        pbuf[0:CONV_HALO, :] = jnp.zeros((CONV_HALO, pbuf.shape[1]), _F32)

    @pl.when((j == 0) | (j % tiles_per_seq == 1 % tiles_per_seq))
    def _():
        state[...] = jnp.zeros_like(state)
        hcarry[...] = jnp.zeros_like(hcarry)

    proj = {}
    _interleave(
        (_project_stage(x_ref, nw_ref, wconv_ref, wgate_ref, wba_ref, pbuf, proj), 2),
        (_lru_stage(xr_s, gate_s, wlru_ref, blru_ref, ap_ref, hcarry, lru_ref), 1),
        (_gdn_recurrence_stage(lhs_s, n_s, o2_s, gam_s, z_s, gnw_ref, state, gdn_ref), 1))

    y = _causal_conv(pbuf, cw_ref, ts)
    pbuf[0:CONV_HALO, :] = pbuf[ts:ts + CONV_HALO, :]

    pba = proj["pba"]
    lane = lax.broadcasted_iota(jnp.int32, pba.shape, 1)
    gb = jnp.where(lane < n_heads, _sigmoid(pba),
                   -jnp.exp(alog_ref[...]) * _softplus(pba + dtb_ref[...]))

    qkv = y[:, :3 * gdn_width]
    qkv = _silu(qkv)
    scale = HEAD_DIM ** -0.5
    q_all, k_all, v_all = [], [], []
    for hd in range(n_heads):
        lo = hd * HEAD_DIM
        qh = qkv[:, lo:lo + HEAD_DIM]
        kh = qkv[:, gdn_width + lo:gdn_width + lo + HEAD_DIM]
        q_all.append(qh * (lax.rsqrt(jnp.sum(qh * qh, axis=-1, keepdims=True) + EPS) * scale))
        k_all.append(kh * lax.rsqrt(jnp.sum(kh * kh, axis=-1, keepdims=True) + EPS))
        v_all.append(qkv[:, 2 * gdn_width + lo:2 * gdn_width + lo + HEAD_DIM])

    problems = []
    for r0 in range(0, ts, GDN_TILE):
        rows = slice(r0, r0 + GDN_TILE)
        gb_rows = gb[rows]
        for hd in range(n_heads):
            problems.append(_GdnProblem(q_all[hd][rows], k_all[hd][rows], v_all[hd][rows],
                                        gb_rows, hd, r0 // GDN_CHUNK))
    _gdn_chunk_operands(problems, n_heads, lhs_s, n_s, o2_s, gam_s)
    pg = proj["pg"]
    z_s[...] = pg[:, :gdn_width]

    xr_s[...] = (y[:, 3 * gdn_width:] + lcb_ref[...]).astype(xr_s.dtype)
    gate_s[...] = pg[:, gdn_width:].astype(gate_s.dtype)


def _mixer(x, nw, wconv, wgate, wba, cw, lcb, alog, dtb, gnw, wlru, blru, ap, *, ts, seq):
    n_tok, d_model = x.shape
    lru_width = lcb.shape[1]
    gdn_width = wgate.shape[1] - lru_width
    n_heads = gdn_width // HEAD_DIM
    n_tiles = n_tok // ts
    n_ops = n_heads * (ts // GDN_CHUNK)
    consts = (nw, wconv, wgate, wba, cw, lcb, alog, dtb, gnw, wlru, blru, ap)
    front_tile = lambda w: pl.BlockSpec((ts, w), lambda j: (jnp.minimum(j, n_tiles - 1), 0))
    back_tile = lambda w: pl.BlockSpec((ts, w), lambda j: (jnp.maximum(j - 1, 0), 0))
    return pl.pallas_call(
        functools.partial(_mixer_kernel, tiles_per_seq=seq // ts),
        out_shape=(jax.ShapeDtypeStruct((n_tok, gdn_width), _BF16),
                   jax.ShapeDtypeStruct((n_tok, lru_width), _BF16)),
        grid=(n_tiles + 1,),
        in_specs=[front_tile(d_model)] + [_const_spec(c.shape) for c in consts],
        out_specs=(back_tile(gdn_width), back_tile(lru_width)),
        scratch_shapes=[pltpu.VMEM((ts + CONV_HALO, wconv.shape[1]), _F32),
                        pltpu.VMEM((ts, gdn_width), _F32),
                        pltpu.VMEM((n_ops, HEAD_DIM + GDN_CHUNK, HEAD_DIM), _BF16),
                        pltpu.VMEM((n_ops, HEAD_DIM, HEAD_DIM), _F32),
                        pltpu.VMEM((n_ops, GDN_CHUNK, HEAD_DIM), _F32),
                        pltpu.VMEM((n_ops, SUBLANES, LANES), _F32),
                        pltpu.VMEM((n_heads, HEAD_DIM, HEAD_DIM), _F32),
                        pltpu.VMEM((ts, lru_width), _BF16),
                        pltpu.VMEM((ts, lru_width), _BF16),
                        pltpu.VMEM((1, lru_width), _F32)],
        compiler_params=pltpu.CompilerParams(
            dimension_semantics=("arbitrary",),
            vmem_limit_bytes=VMEM_LIMIT_BYTES),
        name="mixer",
    )(x, *consts)


def _block_diag_dot(xb, w_ref, col0):
    k = xb.shape[1]
    return jnp.concatenate(
        [jnp.dot(xb[:, t:t + MXU_TILE], w_ref[t:t + MXU_TILE, col0 + t:col0 + t + MXU_TILE],
                 preferred_element_type=_F32) for t in range(0, k, MXU_TILE)], axis=1)


def _lru_stage(xr_ref, gate_ref, wlru_ref, blru_ref, ap_ref, hcarry, lru_ref):
    ts, lru_width = xr_ref.shape
    nsp = _softplus(-ap_ref[...])
    row = lax.broadcasted_iota(jnp.int32, (LRU_ROWS // SUBLANES, SUBLANES, lru_width), 1)
    carry = hcarry[...]
    for b0 in range(0, ts, LRU_ROWS):
        xb = xr_ref[b0:b0 + LRU_ROWS, :]
        xr = xb.astype(_F32)
        r = _sigmoid(_block_diag_dot(xb, wlru_ref, 0) + blru_ref[:, :lru_width])
        log_a = -LRU_C * r * nsp
        a = jnp.exp(log_a)
        yield
        i = _sigmoid(_block_diag_dot(xb, wlru_ref, lru_width) + blru_ref[:, lru_width:])
        v = jnp.maximum(1.0 - a * a, 0.0)
        b = v * lax.rsqrt(jnp.maximum(v, F32_TINY)) * (i * xr)
        yield
        a = a.reshape(LRU_ROWS // SUBLANES, SUBLANES, lru_width)
        b = b.reshape(LRU_ROWS // SUBLANES, SUBLANES, lru_width)
        d = 1
        while d < SUBLANES:
            keep = row >= d
            a_sh = jnp.where(keep, pltpu.roll(a, d, 1), 1.0)
            b_sh = jnp.where(keep, pltpu.roll(b, d, 1), 0.0)
            b = a * b_sh + b
            a = a * a_sh
            d *= 2
        a = a.reshape(LRU_ROWS, lru_width)
        b = b.reshape(LRU_ROWS, lru_width)
        yield
        groups = []
        for r0 in range(0, LRU_ROWS, SUBLANES):
            hg = b[r0:r0 + SUBLANES] + a[r0:r0 + SUBLANES] * carry
            carry = hg[SUBLANES - 1:SUBLANES]
            groups.append(hg)
        gate = gate_ref[b0:b0 + LRU_ROWS, :].astype(_F32)
        lru_ref[b0:b0 + LRU_ROWS, :] = (jnp.concatenate(groups, axis=0)
                                        * _gelu_tanh(gate)).astype(lru_ref.dtype)
        yield
    hcarry[...] = carry


def _mlp_kernel(x_ref, gdn_ref, lru_ref, wog_ref, wol_ref, nw_ref, w1_ref, w2_ref, fnw_ref,
                o_ref, *, ff_chunk):
    d_ff = w1_ref.shape[1]
    x1 = (x_ref[...]
          + jnp.dot(gdn_ref[...], wog_ref[...], preferred_element_type=_F32)
          + jnp.dot(lru_ref[...], wol_ref[...], preferred_element_type=_F32))
    mb = _rmsnorm(x1, nw_ref[...]).astype(_BF16)
    ff = None
    for c0 in range(0, d_ff, ff_chunk):
        u = jnp.maximum(
            jnp.dot(mb, w1_ref[:, c0:c0 + ff_chunk], preferred_element_type=_F32), 0.0)
        t = jnp.dot((u * u).astype(_BF16), w2_ref[c0:c0 + ff_chunk, :],
                    preferred_element_type=_F32)
        ff = t if ff is None else ff + t
    o_ref[...] = _rmsnorm(x1 + ff, fnw_ref[...])


def _mlp(x, gdn, lru, wog, wol, nw, w1, w2, fnw, *, tm, ff_chunk):
    n_tok, d_model = x.shape
    tile = lambda w: pl.BlockSpec((tm, w), lambda i: (i, 0))
    consts = (wog, wol, nw, w1, w2, fnw)
    return pl.pallas_call(
        functools.partial(_mlp_kernel, ff_chunk=ff_chunk),
        out_shape=jax.ShapeDtypeStruct((n_tok, d_model), _F32),
        grid=(n_tok // tm,),
        in_specs=[tile(d_model), tile(gdn.shape[1]), tile(lru.shape[1])]
                 + [_const_spec(c.shape) for c in consts],
        out_specs=tile(d_model),
        compiler_params=pltpu.CompilerParams(
            dimension_semantics=("parallel",),
            vmem_limit_bytes=VMEM_LIMIT_BYTES),
        name="outproj_mlp",
    )(x, gdn, lru, *consts)


def _block_diag(w):
    g, i, j = w.shape
    eye = jnp.eye(g, dtype=w.dtype)
    return (eye[:, None, :, None] * w[:, :, None, :]).reshape(g * i, g * j)


def _layer(x, norm_mix_w, w_in, gdn_conv_w, gdn_A_log, gdn_dt_bias, gdn_norm_w,
           lru_conv_w, lru_conv_b, lru_gate_a_w, lru_gate_a_b, lru_gate_x_w, lru_gate_x_b,
           lru_a_param, w_out, norm_mlp_w, w_ff1, w_ff2, final_norm_w, *, seq_tile, tok_tile,
           ff_chunk):
    bsz, seq, d_model = x.shape
    n_heads = gdn_A_log.shape[0]
    gdn_width = n_heads * HEAD_DIM
    lru_width = lru_conv_b.shape[0]
    assert 2 * n_heads <= LANES
    assert seq % seq_tile == 0 and seq_tile % GDN_TILE == 0 and GDN_TILE % GDN_CHUNK == 0
    assert seq_tile % LRU_ROWS == 0
    assert (bsz * seq) % tok_tile == 0 and w_ff1.shape[1] % ff_chunk == 0
    assert MXU_TILE % lru_gate_a_w.shape[1] == 0 and lru_width % MXU_TILE == 0

    o_qkv, o_z = 0, 3 * gdn_width
    o_b = o_z + gdn_width
    o_a = o_b + n_heads
    o_lx = o_a + n_heads
    o_lg = o_lx + lru_width
    wconv = jnp.concatenate([w_in[:, o_qkv:o_z], w_in[:, o_lx:o_lg]], axis=1).astype(_BF16)
    wgate = jnp.concatenate([w_in[:, o_z:o_b], w_in[:, o_lg:o_lg + lru_width]], axis=1).astype(_BF16)
    wba = jnp.pad(w_in[:, o_b:o_lx], ((0, 0), (0, LANES - 2 * n_heads))).astype(_BF16)
    cw = jnp.concatenate([gdn_conv_w, lru_conv_w], axis=1).astype(_F32)
    pad_heads = lambda p: jnp.pad(p.astype(_F32), (n_heads, LANES - 2 * n_heads)).reshape(1, LANES)
    wlru = jnp.concatenate([_block_diag(lru_gate_a_w), _block_diag(lru_gate_x_w)], axis=1).astype(_BF16)
    blru = jnp.concatenate([lru_gate_a_b.reshape(1, -1), lru_gate_x_b.reshape(1, -1)], axis=1)

    n_tok = bsz * seq
    x2d = x.reshape(n_tok, d_model)
    gdn, lru = _mixer(
        x2d, norm_mix_w.reshape(1, -1), wconv, wgate, wba, cw, lru_conv_b.reshape(1, -1),
        pad_heads(gdn_A_log), pad_heads(gdn_dt_bias), gdn_norm_w.reshape(1, -1),
        wlru, blru, lru_a_param.reshape(1, -1), ts=seq_tile, seq=seq)

    out = _mlp(x2d, gdn, lru,
               w_out[:gdn_width].astype(_BF16), w_out[gdn_width:].astype(_BF16),
               norm_mlp_w.reshape(1, -1), w_ff1.astype(_BF16), w_ff2.astype(_BF16),
               final_norm_w.reshape(1, -1), tm=tok_tile, ff_chunk=ff_chunk)
    return out.reshape(bsz, seq, d_model)


def kernel(x, norm_mix_w, w_in, gdn_conv_w, gdn_A_log, gdn_dt_bias, gdn_norm_w, lru_conv_w, lru_conv_b, lru_gate_a_w, lru_gate_a_b, lru_gate_x_w, lru_gate_x_b, lru_a_param, w_out, norm_mlp_w, w_ff1, w_ff2, final_norm_w):
    assert norm_mix_w.shape[0] == 1, "single-layer stack"
    return _layer(x, norm_mix_w[0], w_in[0], gdn_conv_w[0], gdn_A_log[0], gdn_dt_bias[0],
                  gdn_norm_w[0], lru_conv_w[0], lru_conv_b[0], lru_gate_a_w[0], lru_gate_a_b[0],
                  lru_gate_x_w[0], lru_gate_x_b[0], lru_a_param[0], w_out[0], norm_mlp_w[0],
                  w_ff1[0], w_ff2[0], final_norm_w, seq_tile=512, tok_tile=1024, ff_chunk=2048)
```

```python
import functools
import math
from typing import NamedTuple

import jax
import jax.numpy as jnp
from jax import lax
from jax.experimental import pallas as pl
from jax.experimental.pallas import tpu as pltpu

EPS = 1e-6
HEAD_DIM = 128
LANES = 128
SUBLANES = 8
GDN_CHUNK = 64
INV_BASE = 16
LRU_C = 8.0
CONV_WIDTH = 4
CONV_HALO = 8
PROJ_COLS = 256
GDN_TILE = 128
LRU_ROWS = 512
F32_TINY = 1.1754944e-38
MXU_TILE = 256
VMEM_LIMIT_BYTES = 56 * 1024 * 1024

_BF16 = jnp.bfloat16
_F32 = jnp.float32
_NT_DIMS = (((1,), (1,)), ((), ()))


def _dot(a, b):
    return jnp.dot(a.astype(_BF16), b.astype(_BF16), preferred_element_type=_F32)


def _dot_nt(a, b):
    return lax.dot_general(a.astype(_BF16), b.astype(_BF16), _NT_DIMS,
                           preferred_element_type=_F32)


def _sigmoid(x):
    return 1.0 / (1.0 + jnp.exp(-x))


def _silu(x):
    hx = 0.5 * x
    return hx * jnp.tanh(hx) + hx


def _softplus(x):
    return jnp.maximum(x, 0.0) + jnp.log1p(jnp.exp(-jnp.abs(x)))


def _gelu_tanh(x):
    c = math.sqrt(2.0 / math.pi)
    hx = 0.5 * x
    return hx * jnp.tanh(x * (c + (c * 0.044715) * (x * x))) + hx


def _rmsnorm(x, w):
    return x * lax.rsqrt(jnp.mean(x * x, axis=-1, keepdims=True) + EPS) * w


def _const_spec(shape):
    nd = len(shape)
    return pl.BlockSpec(shape, lambda *_: (0,) * nd, pipeline_mode=pl.Buffered(1))


def _interleave(*stages):
    live = list(stages)
    while live:
        for entry in list(live):
            stage, pieces = entry
            for _ in range(pieces):
                try:
                    next(stage)
                except StopIteration:
                    live.remove(entry)
                    break


def _causal_conv(pbuf, cw_ref, ts):
    pe = pbuf[...]
    pd = pltpu.roll(pe, 1, 0)
    near = cw_ref[3:4, :] * pe + cw_ref[2:3, :] * pd
    far = cw_ref[1:2, :] * pe + cw_ref[0:1, :] * pd
    y = near + pltpu.roll(far, 2, 0)
    return y[CONV_HALO:CONV_HALO + ts, :]


def _pack_chunks(m, lane_chunk):
    n = m.shape[0] // GDN_CHUNK
    out = m[(n - 1) * GDN_CHUNK:, :]
    for c in range(n - 2, -1, -1):
        out = jnp.where(lane_chunk == c, m[c * GDN_CHUNK:(c + 1) * GDN_CHUNK, :], out)
    return out


def _block_diag_of(p, bd_mask):
    pb = p.astype(_BF16)
    return jnp.concatenate([pb] * (bd_mask.shape[0] // GDN_CHUNK), axis=0) * bd_mask


def _packed_dot(x, p_bd):
    return jnp.dot(x.astype(_BF16), p_bd, preferred_element_type=_F32)


def _unit_lower_inverse(a_heads, r, j, bd_mask):
    c = GDN_CHUNK
    eye = (r == j).astype(_F32)
    base = (r // INV_BASE) == (j // INV_BASE)
    d = [jnp.where(base, a, 0.0) for a in a_heads]
    x = [eye - dh for dh in d]
    p = [_packed_dot(dh, _block_diag_of(dh, bd_mask)) for dh in d]
    span = 4
    while span < INV_BASE:
        both = [_packed_dot(jnp.concatenate([xh, ph], axis=0), _block_diag_of(ph, bd_mask))
                for xh, ph in zip(x, p)]
        x = [xh + bh[:c] for xh, bh in zip(x, both)]
        p = [bh[c:] for bh in both]
        span *= 2
    x = [xh + _packed_dot(xh, _block_diag_of(ph, bd_mask)) for xh, ph in zip(x, p)]
    size = INV_BASE
    while size < GDN_CHUNK:
        off = ((r // (2 * size)) == (j // (2 * size))) & ((r // size) != (j // size))
        xe = [_packed_dot(xh, _block_diag_of(jnp.where(off, a, 0.0), bd_mask))
              for xh, a in zip(x, a_heads)]
        x = [xh - _packed_dot(xeh, _block_diag_of(xh, bd_mask)) for xh, xeh in zip(x, xe)]
        size *= 2
    return x


class _GdnProblem(NamedTuple):
    q: jax.Array
    k: jax.Array
    v: jax.Array
    gb: jax.Array
    head: int
    chunk0: int


def _gdn_chunk_operands(problems, n_heads, lhs_s, n_s, o2_s, gam_s):
    ts = GDN_TILE
    n_chunks = ts // GDN_CHUNK
    chunks_per_head = lhs_s.shape[0] // n_heads
    row = lax.broadcasted_iota(jnp.int32, (ts, ts), 0)
    col = lax.broadcasted_iota(jnp.int32, (ts, ts), 1)
    same_chunk = (row // GDN_CHUNK) == (col // GDN_CHUNK)
    bd_mask = jnp.where(same_chunk, 1.0, 0.0).astype(_BF16)
    r = lax.broadcasted_iota(jnp.int32, (GDN_CHUNK, ts), 0)
    lane = lax.broadcasted_iota(jnp.int32, (GDN_CHUNK, ts), 1)
    lane_chunk = lane // GDN_CHUNK
    j = lane % GDN_CHUNK
    causal = r >= j
    strict = r > j

    ones_tri = jnp.where(same_chunk & (row >= col), 1.0, 0.0).astype(_BF16)
    cumsums = {}
    for p in problems:
        if p.chunk0 not in cumsums:
            gb_hi = p.gb.astype(_BF16)
            gb_r1 = p.gb - gb_hi.astype(_F32)
            gb_mid = gb_r1.astype(_BF16)
            gb_lo = (gb_r1 - gb_mid.astype(_F32)).astype(_BF16)
            gcum = (jnp.dot(ones_tri, gb_hi, preferred_element_type=_F32)
                    + jnp.dot(ones_tri, gb_mid, preferred_element_type=_F32)
                    + jnp.dot(ones_tri, gb_lo, preferred_element_type=_F32))
            cumsums[p.chunk0] = (gcum, gcum.T)

    gcol_all, a_all, rhs_all, qk_all, qg_all = [], [], [], [], []
    for p in problems:
        gcum, gcum_t = cumsums[p.chunk0]
        beta = p.gb[:, p.head:p.head + 1]
        gcol = gcum[:, n_heads + p.head:n_heads + p.head + 1]
        grow = gcum_t[n_heads + p.head:n_heads + p.head + 1, :]
        gpk = _pack_chunks(jnp.broadcast_to(gcol, (ts, ts)), lane_chunk)
        decay = jnp.where(causal, jnp.exp(jnp.where(causal, gpk - grow, 0.0)), 0.0)
        egc = jnp.exp(gcol)
        kb = p.k * beta
        gcol_all.append(gcol)
        a_all.append(jnp.where(strict, _pack_chunks(_dot_nt(kb, p.k), lane_chunk) * decay, 0.0))
        rhs_all.append(jnp.concatenate([p.v * beta, kb * egc], axis=1))
        qk_all.append(_pack_chunks(_dot_nt(p.q, p.k), lane_chunk) * decay)
        qg_all.append(p.q * egc)
    t_all = _unit_lower_inverse(a_all, r, j, bd_mask)
    sol_all = [jnp.dot(_block_diag_of(t, bd_mask), rhs.astype(_BF16), preferred_element_type=_F32)
               for t, rhs in zip(t_all, rhs_all)]

    for c in range(n_chunks):
        r0 = c * GDN_CHUNK
        r1 = r0 + GDN_CHUNK
        for i, p in enumerate(problems):
            idx = p.head * chunks_per_head + p.chunk0 + c
            sol = sol_all[i][r0:r1]
            g_last = gcol_all[i][r1 - 1:r1, :]
            k_dec = p.k[r0:r1] * jnp.exp(g_last - gcol_all[i][r0:r1])
            kd_sol = _dot(k_dec.T, sol)
            qk_sol = _dot(qk_all[i][:, r0:r1], sol)
            lhs_s[idx] = jnp.concatenate(
                [kd_sol[:, HEAD_DIM:], qg_all[i][r0:r1] - qk_sol[:, HEAD_DIM:]],
                axis=0).astype(lhs_s.dtype)
            n_s[idx] = kd_sol[:, :HEAD_DIM]
            o2_s[idx] = qk_sol[:, :HEAD_DIM]
            gam_s[idx] = jnp.broadcast_to(jnp.exp(g_last), gam_s.shape[1:])


def _gdn_recurrence_stage(lhs_s, n_s, o2_s, gam_s, z_ref, gnw_ref, state, gdn_ref):
    n_heads = state.shape[0]
    heads = range(n_heads)
    n_chunks = lhs_s.shape[0] // n_heads
    s_all = [state[hd] for hd in heads]
    outs = [[] for _ in heads]
    for c in range(n_chunks):
        for hd in heads:
            idx = hd * n_chunks + c
            s = s_all[hd]
            r = jnp.dot(lhs_s[idx], s.astype(_BF16), preferred_element_type=_F32)
            outs[hd].append(r[HEAD_DIM:] + o2_s[idx])
            s_all[hd] = s * gam_s[idx][0:1, :] - r[:HEAD_DIM] + n_s[idx]
        yield
    for hd in heads:
        lo = hd * HEAD_DIM
        state[hd] = s_all[hd]
        z = z_ref[:, lo:lo + HEAD_DIM]
        o = _rmsnorm(jnp.concatenate(outs[hd], axis=0), gnw_ref[...]) * _silu(z)
        gdn_ref[:, lo:lo + HEAD_DIM] = o.astype(gdn_ref.dtype)
        yield


def _project_stage(x_ref, nw_ref, wconv_ref, wgate_ref, wba_ref, pbuf, out):
    ts = x_ref.shape[0]
    hb = _rmsnorm(x_ref[...], nw_ref[...]).astype(_BF16)
    yield
    for c0 in range(0, wconv_ref.shape[1], PROJ_COLS):
        pbuf[CONV_HALO:CONV_HALO + ts, c0:c0 + PROJ_COLS] = jnp.dot(
            hb, wconv_ref[:, c0:c0 + PROJ_COLS], preferred_element_type=_F32)
        yield
    parts = []
    for c0 in range(0, wgate_ref.shape[1], PROJ_COLS):
        parts.append(jnp.dot(hb, wgate_ref[:, c0:c0 + PROJ_COLS], preferred_element_type=_F32))
        yield
    out["pg"] = jnp.concatenate(parts, axis=1)
    out["pba"] = jnp.dot(hb, wba_ref[...], preferred_element_type=_F32)


def _mixer_kernel(x_ref, nw_ref, wconv_ref, wgate_ref, wba_ref, cw_ref, lcb_ref,
                  alog_ref, dtb_ref, gnw_ref, wlru_ref, blru_ref, ap_ref,
                  gdn_ref, lru_ref,
                  pbuf, z_s, lhs_s, n_s, o2_s, gam_s, state, xr_s, gate_s, hcarry, *,
                  tiles_per_seq):
    j = pl.program_id(0)
    ts = x_ref.shape[0]
    gdn_width = gdn_ref.shape[1]
    n_heads = gdn_width // HEAD_DIM

    @pl.when(j == 0)
    def _():
        for ref in (z_s, lhs_s, n_s, o2_s, gam_s, xr_s, gate_s):
            ref[...] = jnp.zeros_like(ref)

    @pl.when(j % tiles_per_seq == 0)
    def _():
        pbuf[0:CONV_HALO, :] = jnp.zeros((CONV_HALO, pbuf.shape[1]), _F32)

    @pl.when((j == 0) | (j % tiles_per_seq == 1 % tiles_per_seq))
    def _():
        state[...] = jnp.zeros_like(state)
        hcarry[...] = jnp.zeros_like(hcarry)

    proj = {}
    _interleave(
        (_project_stage(x_ref, nw_ref, wconv_ref, wgate_ref, wba_ref, pbuf, proj), 2),
        (_lru_stage(xr_s, gate_s, wlru_ref, blru_ref, ap_ref, hcarry, lru_ref), 1),
        (_gdn_recurrence_stage(lhs_s, n_s, o2_s, gam_s, z_s, gnw_ref, state, gdn_ref), 1))

    y = _causal_conv(pbuf, cw_ref, ts)
    pbuf[0:CONV_HALO, :] = pbuf[ts:ts + CONV_HALO, :]

    pba = proj["pba"]
    lane = lax.broadcasted_iota(jnp.int32, pba.shape, 1)
    gb = jnp.where(lane < n_heads, _sigmoid(pba),
                   -jnp.exp(alog_ref[...]) * _softplus(pba + dtb_ref[...]))

    qkv = y[:, :3 * gdn_width]
    qkv = _silu(qkv)
    scale = HEAD_DIM ** -0.5
    q_all, k_all, v_all = [], [], []
    for hd in range(n_heads):
        lo = hd * HEAD_DIM
        qh = qkv[:, lo:lo + HEAD_DIM]
        kh = qkv[:, gdn_width + lo:gdn_width + lo + HEAD_DIM]
        q_all.append(qh * (lax.rsqrt(jnp.sum(qh * qh, axis=-1, keepdims=True) + EPS) * scale))
        k_all.append(kh * lax.rsqrt(jnp.sum(kh * kh, axis=-1, keepdims=True) + EPS))
        v_all.append(qkv[:, 2 * gdn_width + lo:2 * gdn_width + lo + HEAD_DIM])

    problems = []
    for r0 in range(0, ts, GDN_TILE):
        rows = slice(r0, r0 + GDN_TILE)
        gb_rows = gb[rows]
        for hd in range(n_heads):
            problems.append(_GdnProblem(q_all[hd][rows], k_all[hd][rows], v_all[hd][rows],
                                        gb_rows, hd, r0 // GDN_CHUNK))
    _gdn_chunk_operands(problems, n_heads, lhs_s, n_s, o2_s, gam_s)
    pg = proj["pg"]
    z_s[...] = pg[:, :gdn_width]

    xr_s[...] = (y[:, 3 * gdn_width:] + lcb_ref[...]).astype(xr_s.dtype)
    gate_s[...] = pg[:, gdn_width:].astype(gate_s.dtype)


def _mixer(x, nw, wconv, wgate, wba, cw, lcb, alog, dtb, gnw, wlru, blru, ap, *, ts, seq):
    n_tok, d_model = x.shape
    lru_width = lcb.shape[1]
    gdn_width = wgate.shape[1] - lru_width
    n_heads = gdn_width // HEAD_DIM
    n_tiles = n_tok // ts
    n_ops = n_heads * (ts // GDN_CHUNK)
    consts = (nw, wconv, wgate, wba, cw, lcb, alog, dtb, gnw, wlru, blru, ap)
    front_tile = lambda w: pl.BlockSpec((ts, w), lambda j: (jnp.minimum(j, n_tiles - 1), 0))
    back_tile = lambda w: pl.BlockSpec((ts, w), lambda j: (jnp.maximum(j - 1, 0), 0))
    return pl.pallas_call(
        functools.partial(_mixer_kernel, tiles_per_seq=seq // ts),
        out_shape=(jax.ShapeDtypeStruct((n_tok, gdn_width), _BF16),
                   jax.ShapeDtypeStruct((n_tok, lru_width), _BF16)),
        grid=(n_tiles + 1,),
        in_specs=[front_tile(d_model)] + [_const_spec(c.shape) for c in consts],
        out_specs=(back_tile(gdn_width), back_tile(lru_width)),
        scratch_shapes=[pltpu.VMEM((ts + CONV_HALO, wconv.shape[1]), _F32),
                        pltpu.VMEM((ts, gdn_width), _F32),
                        pltpu.VMEM((n_ops, HEAD_DIM + GDN_CHUNK, HEAD_DIM), _BF16),
                        pltpu.VMEM((n_ops, HEAD_DIM, HEAD_DIM), _F32),
                        pltpu.VMEM((n_ops, GDN_CHUNK, HEAD_DIM), _F32),
                        pltpu.VMEM((n_ops, SUBLANES, LANES), _F32),
                        pltpu.VMEM((n_heads, HEAD_DIM, HEAD_DIM), _F32),
                        pltpu.VMEM((ts, lru_width), _BF16),
                        pltpu.VMEM((ts, lru_width), _BF16),
                        pltpu.VMEM((1, lru_width), _F32)],
        compiler_params=pltpu.CompilerParams(
            dimension_semantics=("arbitrary",),
            vmem_limit_bytes=VMEM_LIMIT_BYTES),
        name="mixer",
    )(x, *consts)


def _block_diag_dot(xb, w_ref, col0):
    k = xb.shape[1]
    return jnp.concatenate(
        [jnp.dot(xb[:, t:t + MXU_TILE], w_ref[t:t + MXU_TILE, col0 + t:col0 + t + MXU_TILE],
                 preferred_element_type=_F32) for t in range(0, k, MXU_TILE)], axis=1)


def _lru_stage(xr_ref, gate_ref, wlru_ref, blru_ref, ap_ref, hcarry, lru_ref):
    ts, lru_width = xr_ref.shape
    nsp = _softplus(-ap_ref[...])
    row = lax.broadcasted_iota(jnp.int32, (LRU_ROWS // SUBLANES, SUBLANES, lru_width), 1)
    carry = hcarry[...]
    for b0 in range(0, ts, LRU_ROWS):
        xb = xr_ref[b0:b0 + LRU_ROWS, :]
        xr = xb.astype(_F32)
        r = _sigmoid(_block_diag_dot(xb, wlru_ref, 0) + blru_ref[:, :lru_width])
        log_a = -LRU_C * r * nsp
        a = jnp.exp(log_a)
        yield
        i = _sigmoid(_block_diag_dot(xb, wlru_ref, lru_width) + blru_ref[:, lru_width:])
        v = jnp.maximum(1.0 - a * a, 0.0)
        b = v * lax.rsqrt(jnp.maximum(v, F32_TINY)) * (i * xr)
        yield
        a = a.reshape(LRU_ROWS // SUBLANES, SUBLANES, lru_width)
        b = b.reshape(LRU_ROWS // SUBLANES, SUBLANES, lru_width)
        d = 1
        while d < SUBLANES:
            keep = row >= d
            a_sh = jnp.where(keep, pltpu.roll(a, d, 1), 1.0)
            b_sh = jnp.where(keep, pltpu.roll(b, d, 1), 0.0)
            b = a * b_sh + b
            a = a * a_sh
            d *= 2
        a = a.reshape(LRU_ROWS, lru_width)
        b = b.reshape(LRU_ROWS, lru_width)
        yield
        groups = []
        for r0 in range(0, LRU_ROWS, SUBLANES):
            hg = b[r0:r0 + SUBLANES] + a[r0:r0 + SUBLANES] * carry
            carry = hg[SUBLANES - 1:SUBLANES]
            groups.append(hg)
        gate = gate_ref[b0:b0 + LRU_ROWS, :].astype(_F32)
        lru_ref[b0:b0 + LRU_ROWS, :] = (jnp.concatenate(groups, axis=0)
                                        * _gelu_tanh(gate)).astype(lru_ref.dtype)
        yield
    hcarry[...] = carry


def _mlp_kernel(x_ref, gdn_ref, lru_ref, wog_ref, wol_ref, nw_ref, w1_ref, w2_ref, fnw_ref,
                o_ref, *, ff_chunk):
    d_ff = w1_ref.shape[1]
    x1 = (x_ref[...]
          + jnp.dot(gdn_ref[...], wog_ref[...], preferred_element_type=_F32)
          + jnp.dot(lru_ref[...], wol_ref[...], preferred_element_type=_F32))
    mb = _rmsnorm(x1, nw_ref[...]).astype(_BF16)
    ff = None
    for c0 in range(0, d_ff, ff_chunk):
        u = jnp.maximum(
            jnp.dot(mb, w1_ref[:, c0:c0 + ff_chunk], preferred_element_type=_F32), 0.0)
        t = jnp.dot((u * u).astype(_BF16), w2_ref[c0:c0 + ff_chunk, :],
                    preferred_element_type=_F32)
        ff = t if ff is None else ff + t
    o_ref[...] = _rmsnorm(x1 + ff, fnw_ref[...])


def _mlp(x, gdn, lru, wog, wol, nw, w1, w2, fnw, *, tm, ff_chunk):
    n_tok, d_model = x.shape
    tile = lambda w: pl.BlockSpec((tm, w), lambda i: (i, 0))
    consts = (wog, wol, nw, w1, w2, fnw)
    return pl.pallas_call(
        functools.partial(_mlp_kernel, ff_chunk=ff_chunk),
        out_shape=jax.ShapeDtypeStruct((n_tok, d_model), _F32),
        grid=(n_tok // tm,),
        in_specs=[tile(d_model), tile(gdn.shape[1]), tile(lru.shape[1])]
                 + [_const_spec(c.shape) for c in consts],
        out_specs=tile(d_model),
        compiler_params=pltpu.CompilerParams(
            dimension_semantics=("parallel",),
            vmem_limit_bytes=VMEM_LIMIT_BYTES),
        name="outproj_mlp",
    )(x, gdn, lru, *consts)


def _block_diag(w):
    g, i, j = w.shape
    eye = jnp.eye(g, dtype=w.dtype)
    return (eye[:, None, :, None] * w[:, :, None, :]).reshape(g * i, g * j)


def _layer(x, norm_mix_w, w_in, gdn_conv_w, gdn_A_log, gdn_dt_bias, gdn_norm_w,
           lru_conv_w, lru_conv_b, lru_gate_a_w, lru_gate_a_b, lru_gate_x_w, lru_gate_x_b,
           lru_a_param, w_out, norm_mlp_w, w_ff1, w_ff2, final_norm_w, *, seq_tile, tok_tile,
           ff_chunk):
    bsz, seq, d_model = x.shape
    n_heads = gdn_A_log.shape[0]
    gdn_width = n_heads * HEAD_DIM
    lru_width = lru_conv_b.shape[0]
    assert 2 * n_heads <= LANES
    assert seq % seq_tile == 0 and seq_tile % GDN_TILE == 0 and GDN_TILE % GDN_CHUNK == 0
    assert seq_tile % LRU_ROWS == 0
    assert (bsz * seq) % tok_tile == 0 and w_ff1.shape[1] % ff_chunk == 0
    assert MXU_TILE % lru_gate_a_w.shape[1] == 0 and lru_width % MXU_TILE == 0

    o_qkv, o_z = 0, 3 * gdn_width
    o_b = o_z + gdn_width
    o_a = o_b + n_heads
    o_lx = o_a + n_heads
    o_lg = o_lx + lru_width
    wconv = jnp.concatenate([w_in[:, o_qkv:o_z], w_in[:, o_lx:o_lg]], axis=1).astype(_BF16)
    wgate = jnp.concatenate([w_in[:, o_z:o_b], w_in[:, o_lg:o_lg + lru_width]], axis=1).astype(_BF16)
    wba = jnp.pad(w_in[:, o_b:o_lx], ((0, 0), (0, LANES - 2 * n_heads))).astype(_BF16)
    cw = jnp.concatenate([gdn_conv_w, lru_conv_w], axis=1).astype(_F32)
    pad_heads = lambda p: jnp.pad(p.astype(_F32), (n_heads, LANES - 2 * n_heads)).reshape(1, LANES)
    wlru = jnp.concatenate([_block_diag(lru_gate_a_w), _block_diag(lru_gate_x_w)], axis=1).astype(_BF16)
    blru = jnp.concatenate([lru_gate_a_b.reshape(1, -1), lru_gate_x_b.reshape(1, -1)], axis=1)

    n_tok = bsz * seq
    x2d = x.reshape(n_tok, d_model)
    gdn, lru = _mixer(
        x2d, norm_mix_w.reshape(1, -1), wconv, wgate, wba, cw, lru_conv_b.reshape(1, -1),
        pad_heads(gdn_A_log), pad_heads(gdn_dt_bias), gdn_norm_w.reshape(1, -1),
        wlru, blru, lru_a_param.reshape(1, -1), ts=seq_tile, seq=seq)

    out = _mlp(x2d, gdn, lru,
               w_out[:gdn_width].astype(_BF16), w_out[gdn_width:].astype(_BF16),
               norm_mlp_w.reshape(1, -1), w_ff1.astype(_BF16), w_ff2.astype(_BF16),
               final_norm_w.reshape(1, -1), tm=tok_tile, ff_chunk=ff_chunk)
    return out.reshape(bsz, seq, d_model)


def kernel(x, norm_mix_w, w_in, gdn_conv_w, gdn_A_log, gdn_dt_bias, gdn_norm_w, lru_conv_w, lru_conv_b, lru_gate_a_w, lru_gate_a_b, lru_gate_x_w, lru_gate_x_b, lru_a_param, w_out, norm_mlp_w, w_ff1, w_ff2, final_norm_w):
    assert norm_mix_w.shape[0] == 1, "single-layer stack"
    return _layer(x, norm_mix_w[0], w_in[0], gdn_conv_w[0], gdn_A_log[0], gdn_dt_bias[0],
                  gdn_norm_w[0], lru_conv_w[0], lru_conv_b[0], lru_gate_a_w[0], lru_gate_a_b[0],
                  lru_gate_x_w[0], lru_gate_x_b[0], lru_a_param[0], w_out[0], norm_mlp_w[0],
                  w_ff1[0], w_ff2[0], final_norm_w, seq_tile=512, tok_tile=1024, ff_chunk=2048)
```

```python
import functools
import math
from typing import NamedTuple

import jax
import jax.numpy as jnp
from jax import lax
from jax.experimental import pallas as pl
from jax.experimental.pallas import tpu as pltpu

EPS = 1e-6
HEAD_DIM = 128
LANES = 128
SUBLANES = 8
GDN_CHUNK = 64
INV_BASE = 16
LRU_C = 8.0
CONV_WIDTH = 4
CONV_HALO = 8
PROJ_COLS = 256
GDN_TILE = 128
LRU_ROWS = 512
MLP_HALVES = 2
MLP_LAG = 3
F32_TINY = 1.1754944e-38
MXU_TILE = 256
VMEM_LIMIT_BYTES = 56 * 1024 * 1024

_BF16 = jnp.bfloat16
_F32 = jnp.float32
_NT_DIMS = (((1,), (1,)), ((), ()))


def _dot(a, b):
    return jnp.dot(a.astype(_BF16), b.astype(_BF16), preferred_element_type=_F32)


def _dot_nt(a, b):
    return lax.dot_general(a.astype(_BF16), b.astype(_BF16), _NT_DIMS,
                           preferred_element_type=_F32)


def _sigmoid(x):
    return 1.0 / (1.0 + jnp.exp(-x))


def _silu(x):
    hx = 0.5 * x
    return hx * jnp.tanh(hx) + hx


def _softplus(x):
    return jnp.maximum(x, 0.0) + jnp.log1p(jnp.exp(-jnp.abs(x)))


def _gelu_tanh(x):
    c = math.sqrt(2.0 / math.pi)
    hx = 0.5 * x
    return hx * jnp.tanh(x * (c + (c * 0.044715) * (x * x))) + hx


def _rmsnorm(x, w):
    return x * lax.rsqrt(jnp.mean(x * x, axis=-1, keepdims=True) + EPS) * w


def _const_spec(shape):
    nd = len(shape)
    return pl.BlockSpec(shape, lambda *_: (0,) * nd, pipeline_mode=pl.Buffered(1))


def _interleave(*stages):
    live = list(stages)
    while live:
        for entry in list(live):
            stage, pieces = entry
            for _ in range(pieces):
                try:
                    next(stage)
                except StopIteration:
                    live.remove(entry)
                    break


def _causal_conv(pbuf, cw_ref, ts):
    pe = pbuf[...]
    pd = pltpu.roll(pe, 1, 0)
    near = cw_ref[3:4, :] * pe + cw_ref[2:3, :] * pd
    far = cw_ref[1:2, :] * pe + cw_ref[0:1, :] * pd
    y = near + pltpu.roll(far, 2, 0)
    return y[CONV_HALO:CONV_HALO + ts, :]


def _pack_chunks(m, lane_chunk):
    n = m.shape[0] // GDN_CHUNK
    out = m[(n - 1) * GDN_CHUNK:, :]
    for c in range(n - 2, -1, -1):
        out = jnp.where(lane_chunk == c, m[c * GDN_CHUNK:(c + 1) * GDN_CHUNK, :], out)
    return out


def _block_diag_of(p, bd_mask):
    pb = p.astype(_BF16)
    return jnp.concatenate([pb] * (bd_mask.shape[0] // GDN_CHUNK), axis=0) * bd_mask


def _packed_dot(x, p_bd):
    return jnp.dot(x.astype(_BF16), p_bd, preferred_element_type=_F32)


def _unit_lower_inverse(a_heads, r, j, bd_mask):
    c = GDN_CHUNK
    eye = (r == j).astype(_F32)
    base = (r // INV_BASE) == (j // INV_BASE)
    d = [jnp.where(base, a, 0.0) for a in a_heads]
    x = [eye - dh for dh in d]
    p = [_packed_dot(dh, _block_diag_of(dh, bd_mask)) for dh in d]
    span = 4
    while span < INV_BASE:
        both = [_packed_dot(jnp.concatenate([xh, ph], axis=0), _block_diag_of(ph, bd_mask))
                for xh, ph in zip(x, p)]
        x = [xh + bh[:c] for xh, bh in zip(x, both)]
        p = [bh[c:] for bh in both]
        span *= 2
    x = [xh + _packed_dot(xh, _block_diag_of(ph, bd_mask)) for xh, ph in zip(x, p)]
    size = INV_BASE
    while size < GDN_CHUNK:
        off = ((r // (2 * size)) == (j // (2 * size))) & ((r // size) != (j // size))
        xe = [_packed_dot(xh, _block_diag_of(jnp.where(off, a, 0.0), bd_mask))
              for xh, a in zip(x, a_heads)]
        x = [xh - _packed_dot(xeh, _block_diag_of(xh, bd_mask)) for xh, xeh in zip(x, xe)]
        size *= 2
    return x


class _GdnProblem(NamedTuple):
    q: jax.Array
    k: jax.Array
    v: jax.Array
    gb: jax.Array
    head: int
    chunk0: int


def _gdn_chunk_operands(problems, n_heads, lhs_s, n_s, o2_s, gam_s):
    ts = GDN_TILE
    n_chunks = ts // GDN_CHUNK
    chunks_per_head = lhs_s.shape[0] // n_heads
    row = lax.broadcasted_iota(jnp.int32, (ts, ts), 0)
    col = lax.broadcasted_iota(jnp.int32, (ts, ts), 1)
    same_chunk = (row // GDN_CHUNK) == (col // GDN_CHUNK)
    bd_mask = jnp.where(same_chunk, 1.0, 0.0).astype(_BF16)
    r = lax.broadcasted_iota(jnp.int32, (GDN_CHUNK, ts), 0)
    lane = lax.broadcasted_iota(jnp.int32, (GDN_CHUNK, ts), 1)
    lane_chunk = lane // GDN_CHUNK
    j = lane % GDN_CHUNK
    causal = r >= j
    strict = r > j

    ones_tri = jnp.where(same_chunk & (row >= col), 1.0, 0.0).astype(_BF16)
    cumsums = {}
    for p in problems:
        if p.chunk0 not in cumsums:
            gb_hi = p.gb.astype(_BF16)
            gb_r1 = p.gb - gb_hi.astype(_F32)
            gb_mid = gb_r1.astype(_BF16)
            gb_lo = (gb_r1 - gb_mid.astype(_F32)).astype(_BF16)
            gcum = (jnp.dot(ones_tri, gb_hi, preferred_element_type=_F32)
                    + jnp.dot(ones_tri, gb_mid, preferred_element_type=_F32)
                    + jnp.dot(ones_tri, gb_lo, preferred_element_type=_F32))
            cumsums[p.chunk0] = (gcum, gcum.T)

    gcol_all, a_all, rhs_all, qk_all, qg_all = [], [], [], [], []
    for p in problems:
        gcum, gcum_t = cumsums[p.chunk0]
        beta = p.gb[:, p.head:p.head + 1]
        gcol = gcum[:, n_heads + p.head:n_heads + p.head + 1]
        grow = gcum_t[n_heads + p.head:n_heads + p.head + 1, :]
        gpk = _pack_chunks(jnp.broadcast_to(gcol, (ts, ts)), lane_chunk)
        decay = jnp.where(causal, jnp.exp(jnp.where(causal, gpk - grow, 0.0)), 0.0)
        egc = jnp.exp(gcol)
        kb = p.k * beta
        gcol_all.append(gcol)
        a_all.append(jnp.where(strict, _pack_chunks(_dot_nt(kb, p.k), lane_chunk) * decay, 0.0))
        rhs_all.append(jnp.concatenate([p.v * beta, kb * egc], axis=1))
        qk_all.append(_pack_chunks(_dot_nt(p.q, p.k), lane_chunk) * decay)
        qg_all.append(p.q * egc)
    t_all = _unit_lower_inverse(a_all, r, j, bd_mask)
    sol_all = [jnp.dot(_block_diag_of(t, bd_mask), rhs.astype(_BF16), preferred_element_type=_F32)
               for t, rhs in zip(t_all, rhs_all)]

    for c in range(n_chunks):
        r0 = c * GDN_CHUNK
        r1 = r0 + GDN_CHUNK
        for i, p in enumerate(problems):
            idx = p.head * chunks_per_head + p.chunk0 + c
            sol = sol_all[i][r0:r1]
            g_last = gcol_all[i][r1 - 1:r1, :]
            k_dec = p.k[r0:r1] * jnp.exp(g_last - gcol_all[i][r0:r1])
            kd_sol = _dot(k_dec.T, sol)
            qk_sol = _dot(qk_all[i][:, r0:r1], sol)
            lhs_s[idx] = jnp.concatenate(
                [kd_sol[:, HEAD_DIM:], qg_all[i][r0:r1] - qk_sol[:, HEAD_DIM:]],
                axis=0).astype(lhs_s.dtype)
            n_s[idx] = kd_sol[:, :HEAD_DIM]
            o2_s[idx] = qk_sol[:, :HEAD_DIM]
            gam_s[idx] = jnp.broadcast_to(jnp.exp(g_last), gam_s.shape[1:])


def _gdn_recurrence_stage(lhs_s, n_s, o2_s, gam_s, z_ref, gnw_ref, state, gdn_ref):
    n_heads = state.shape[0]
    heads = range(n_heads)
    n_chunks = lhs_s.shape[0] // n_heads
    s_all = [state[hd] for hd in heads]
    outs = [[] for _ in heads]
    for c in range(n_chunks):
        for hd in heads:
            idx = hd * n_chunks + c
            s = s_all[hd]
            r = jnp.dot(lhs_s[idx], s.astype(_BF16), preferred_element_type=_F32)
            outs[hd].append(r[HEAD_DIM:] + o2_s[idx])
            s_all[hd] = s * gam_s[idx][0:1, :] - r[:HEAD_DIM] + n_s[idx]
        yield
    for hd in heads:
        lo = hd * HEAD_DIM
        state[hd] = s_all[hd]
        z = z_ref[:, lo:lo + HEAD_DIM]
        o = _rmsnorm(jnp.concatenate(outs[hd], axis=0), gnw_ref[...]) * _silu(z)
        gdn_ref[:, lo:lo + HEAD_DIM] = o.astype(gdn_ref.dtype)
        yield


def _project_stage(x_ref, nw_ref, wconv_ref, wgate_ref, wba_ref, pbuf, out):
    ts = x_ref.shape[0]
    hb = _rmsnorm(x_ref[...], nw_ref[...]).astype(_BF16)
    yield
    for c0 in range(0, wconv_ref.shape[1], PROJ_COLS):
        pbuf[CONV_HALO:CONV_HALO + ts, c0:c0 + PROJ_COLS] = jnp.dot(
            hb, wconv_ref[:, c0:c0 + PROJ_COLS], preferred_element_type=_F32)
        yield
    parts = []
    for c0 in range(0, wgate_ref.shape[1], PROJ_COLS):
        parts.append(jnp.dot(hb, wgate_ref[:, c0:c0 + PROJ_COLS], preferred_element_type=_F32))
        yield
    out["pg"] = jnp.concatenate(parts, axis=1)
    out["pba"] = jnp.dot(hb, wba_ref[...], preferred_element_type=_F32)


def _mixer_kernel(x_ref, nw_ref, wconv_ref, wgate_ref, wba_ref, cw_ref, lcb_ref,
                  alog_ref, dtb_ref, gnw_ref, wlru_ref, blru_ref, ap_ref,
                  gdn_ref, lru_ref,
                  pbuf, z_s, lhs_s, n_s, o2_s, gam_s, state, xr_s, gate_s, hcarry, *,
                  tiles_per_seq):
    j = pl.program_id(0)
    ts = x_ref.shape[0]
    gdn_width = gdn_ref.shape[1]
    n_heads = gdn_width // HEAD_DIM

    @pl.when(j == 0)
    def _():
        for ref in (z_s, lhs_s, n_s, o2_s, gam_s, xr_s, gate_s):
            ref[...] = jnp.zeros_like(ref)

    @pl.when(j % tiles_per_seq == 0)
    def _():
        pbuf[0:CONV_HALO, :] = jnp.zeros((CONV_HALO, pbuf.shape[1]), _F32)

    @pl.when((j == 0) | (j % tiles_per_seq == 1 % tiles_per_seq))
    def _():
        state[...] = jnp.zeros_like(state)
        hcarry[...] = jnp.zeros_like(hcarry)

    proj = {}
    _interleave(
        (_project_stage(x_ref, nw_ref, wconv_ref, wgate_ref, wba_ref, pbuf, proj), 2),
        (_lru_stage(xr_s, gate_s, wlru_ref, blru_ref, ap_ref, hcarry, lru_ref), 1),
        (_gdn_recurrence_stage(lhs_s, n_s, o2_s, gam_s, z_s, gnw_ref, state, gdn_ref), 1))

    y = _causal_conv(pbuf, cw_ref, ts)
    pbuf[0:CONV_HALO, :] = pbuf[ts:ts + CONV_HALO, :]

    pba = proj["pba"]
    lane = lax.broadcasted_iota(jnp.int32, pba.shape, 1)
    gb = jnp.where(lane < n_heads, _sigmoid(pba),
                   -jnp.exp(alog_ref[...]) * _softplus(pba + dtb_ref[...]))

    qkv = y[:, :3 * gdn_width]
    qkv = _silu(qkv)
    scale = HEAD_DIM ** -0.5
    q_all, k_all, v_all = [], [], []
    for hd in range(n_heads):
        lo = hd * HEAD_DIM
        qh = qkv[:, lo:lo + HEAD_DIM]
        kh = qkv[:, gdn_width + lo:gdn_width + lo + HEAD_DIM]
        q_all.append(qh * (lax.rsqrt(jnp.sum(qh * qh, axis=-1, keepdims=True) + EPS) * scale))
        k_all.append(kh * lax.rsqrt(jnp.sum(kh * kh, axis=-1, keepdims=True) + EPS))
        v_all.append(qkv[:, 2 * gdn_width + lo:2 * gdn_width + lo + HEAD_DIM])

    problems = []
    for r0 in range(0, ts, GDN_TILE):
        rows = slice(r0, r0 + GDN_TILE)
        gb_rows = gb[rows]
        for hd in range(n_heads):
            problems.append(_GdnProblem(q_all[hd][rows], k_all[hd][rows], v_all[hd][rows],
                                        gb_rows, hd, r0 // GDN_CHUNK))
    _gdn_chunk_operands(problems, n_heads, lhs_s, n_s, o2_s, gam_s)
    pg = proj["pg"]
    z_s[...] = pg[:, :gdn_width]

    xr_s[...] = (y[:, 3 * gdn_width:] + lcb_ref[...]).astype(xr_s.dtype)
    gate_s[...] = pg[:, gdn_width:].astype(gate_s.dtype)


def _mixer(x, nw, wconv, wgate, wba, cw, lcb, alog, dtb, gnw, wlru, blru, ap, *, ts, seq):
    n_tok, d_model = x.shape
    lru_width = lcb.shape[1]
    gdn_width = wgate.shape[1] - lru_width
    n_heads = gdn_width // HEAD_DIM
    n_tiles = n_tok // ts
    n_ops = n_heads * (ts // GDN_CHUNK)
    consts = (nw, wconv, wgate, wba, cw, lcb, alog, dtb, gnw, wlru, blru, ap)
    front_tile = lambda w: pl.BlockSpec((ts, w), lambda j: (jnp.minimum(j, n_tiles - 1), 0))
    back_tile = lambda w: pl.BlockSpec((ts, w), lambda j: (jnp.maximum(j - 1, 0), 0))
    return pl.pallas_call(
        functools.partial(_mixer_kernel, tiles_per_seq=seq // ts),
        out_shape=(jax.ShapeDtypeStruct((n_tok, gdn_width), _BF16),
                   jax.ShapeDtypeStruct((n_tok, lru_width), _BF16)),
        grid=(n_tiles + 1,),
        in_specs=[front_tile(d_model)] + [_const_spec(c.shape) for c in consts],
        out_specs=(back_tile(gdn_width), back_tile(lru_width)),
        scratch_shapes=[pltpu.VMEM((ts + CONV_HALO, wconv.shape[1]), _F32),
                        pltpu.VMEM((ts, gdn_width), _F32),
                        pltpu.VMEM((n_ops, HEAD_DIM + GDN_CHUNK, HEAD_DIM), _BF16),
                        pltpu.VMEM((n_ops, HEAD_DIM, HEAD_DIM), _F32),
                        pltpu.VMEM((n_ops, GDN_CHUNK, HEAD_DIM), _F32),
                        pltpu.VMEM((n_ops, SUBLANES, LANES), _F32),
                        pltpu.VMEM((n_heads, HEAD_DIM, HEAD_DIM), _F32),
                        pltpu.VMEM((ts, lru_width), _BF16),
                        pltpu.VMEM((ts, lru_width), _BF16),
                        pltpu.VMEM((1, lru_width), _F32)],
        compiler_params=pltpu.CompilerParams(
            dimension_semantics=("arbitrary",),
            vmem_limit_bytes=VMEM_LIMIT_BYTES),
        name="mixer",
    )(x, *consts)


def _block_diag_dot(xb, w_ref, col0):
    k = xb.shape[1]
    return jnp.concatenate(
        [jnp.dot(xb[:, t:t + MXU_TILE], w_ref[t:t + MXU_TILE, col0 + t:col0 + t + MXU_TILE],
                 preferred_element_type=_F32) for t in range(0, k, MXU_TILE)], axis=1)


def _lru_stage(xr_ref, gate_ref, wlru_ref, blru_ref, ap_ref, hcarry, lru_ref):
    ts, lru_width = xr_ref.shape
    nsp = _softplus(-ap_ref[...])
    row = lax.broadcasted_iota(jnp.int32, (LRU_ROWS // SUBLANES, SUBLANES, lru_width), 1)
    carry = hcarry[...]
    for b0 in range(0, ts, LRU_ROWS):
        xb = xr_ref[b0:b0 + LRU_ROWS, :]
        xr = xb.astype(_F32)
        r = _sigmoid(_block_diag_dot(xb, wlru_ref, 0) + blru_ref[:, :lru_width])
        log_a = -LRU_C * r * nsp
        a = jnp.exp(log_a)
        yield
        i = _sigmoid(_block_diag_dot(xb, wlru_ref, lru_width) + blru_ref[:, lru_width:])
        v = jnp.maximum(1.0 - a * a, 0.0)
        b = v * lax.rsqrt(jnp.maximum(v, F32_TINY)) * (i * xr)
        yield
        a = a.reshape(LRU_ROWS // SUBLANES, SUBLANES, lru_width)
        b = b.reshape(LRU_ROWS // SUBLANES, SUBLANES, lru_width)
        d = 1
        while d < SUBLANES:
            keep = row >= d
            a_sh = jnp.where(keep, pltpu.roll(a, d, 1), 1.0)
            b_sh = jnp.where(keep, pltpu.roll(b, d, 1), 0.0)
            b = a * b_sh + b
            a = a * a_sh
            d *= 2
        a = a.reshape(LRU_ROWS, lru_width)
        b = b.reshape(LRU_ROWS, lru_width)
        yield
        groups = []
        for r0 in range(0, LRU_ROWS, SUBLANES):
            hg = b[r0:r0 + SUBLANES] + a[r0:r0 + SUBLANES] * carry
            carry = hg[SUBLANES - 1:SUBLANES]
            groups.append(hg)
        gate = gate_ref[b0:b0 + LRU_ROWS, :].astype(_F32)
        lru_ref[b0:b0 + LRU_ROWS, :] = (jnp.concatenate(groups, axis=0)
                                        * _gelu_tanh(gate)).astype(lru_ref.dtype)
        yield
    hcarry[...] = carry


def _mlp_rows_stage(rows, lag, x_ref, gdn_ref, lru_ref, wog_ref, wol_ref, nw_ref, w1_ref, w2_ref,
                    fnw_ref, o_ref, ff_chunk):
    for _ in range(lag):
        yield
    d_ff = w1_ref.shape[1]
    x1 = (x_ref[rows, :]
          + jnp.dot(gdn_ref[rows, :], wog_ref[...], preferred_element_type=_F32)
          + jnp.dot(lru_ref[rows, :], wol_ref[...], preferred_element_type=_F32))
    mb = _rmsnorm(x1, nw_ref[...]).astype(_BF16)
    yield
    ff = None
    for c0 in range(0, d_ff, ff_chunk):
        u = jnp.maximum(
            jnp.dot(mb, w1_ref[:, c0:c0 + ff_chunk], preferred_element_type=_F32), 0.0)
        yield
        t = jnp.dot((u * u).astype(_BF16), w2_ref[c0:c0 + ff_chunk, :],
                    preferred_element_type=_F32)
        ff = t if ff is None else ff + t
        yield
    o_ref[rows, :] = _rmsnorm(x1 + ff, fnw_ref[...])


def _mlp_kernel(x_ref, gdn_ref, lru_ref, wog_ref, wol_ref, nw_ref, w1_ref, w2_ref, fnw_ref,
                o_ref, *, ff_chunk):
    rows_per = x_ref.shape[0] // MLP_HALVES
    _interleave(*[
        (_mlp_rows_stage(slice(h * rows_per, (h + 1) * rows_per), h * MLP_LAG, x_ref, gdn_ref,
                         lru_ref, wog_ref, wol_ref, nw_ref, w1_ref, w2_ref, fnw_ref, o_ref,
                         ff_chunk), 1)
        for h in range(MLP_HALVES)])


def _mlp(x, gdn, lru, wog, wol, nw, w1, w2, fnw, *, tm, ff_chunk):
    n_tok, d_model = x.shape
    tile = lambda w: pl.BlockSpec((tm, w), lambda i: (i, 0))
    consts = (wog, wol, nw, w1, w2, fnw)
    return pl.pallas_call(
        functools.partial(_mlp_kernel, ff_chunk=ff_chunk),
        out_shape=jax.ShapeDtypeStruct((n_tok, d_model), _F32),
        grid=(n_tok // tm,),
        in_specs=[tile(d_model), tile(gdn.shape[1]), tile(lru.shape[1])]
                 + [_const_spec(c.shape) for c in consts],
        out_specs=tile(d_model),
        compiler_params=pltpu.CompilerParams(
            dimension_semantics=("parallel",),
            vmem_limit_bytes=VMEM_LIMIT_BYTES),
        name="outproj_mlp",
    )(x, gdn, lru, *consts)


def _block_diag(w):
    g, i, j = w.shape
    eye = jnp.eye(g, dtype=w.dtype)
    return (eye[:, None, :, None] * w[:, :, None, :]).reshape(g * i, g * j)


def _layer(x, norm_mix_w, w_in, gdn_conv_w, gdn_A_log, gdn_dt_bias, gdn_norm_w,
           lru_conv_w, lru_conv_b, lru_gate_a_w, lru_gate_a_b, lru_gate_x_w, lru_gate_x_b,
           lru_a_param, w_out, norm_mlp_w, w_ff1, w_ff2, final_norm_w, *, seq_tile, tok_tile,
           ff_chunk):
    bsz, seq, d_model = x.shape
    n_heads = gdn_A_log.shape[0]
    gdn_width = n_heads * HEAD_DIM
    lru_width = lru_conv_b.shape[0]
    assert 2 * n_heads <= LANES
    assert seq % seq_tile == 0 and seq_tile % GDN_TILE == 0 and GDN_TILE % GDN_CHUNK == 0
    assert seq_tile % LRU_ROWS == 0
    assert (bsz * seq) % tok_tile == 0 and w_ff1.shape[1] % ff_chunk == 0
    assert MXU_TILE % lru_gate_a_w.shape[1] == 0 and lru_width % MXU_TILE == 0

    o_qkv, o_z = 0, 3 * gdn_width
    o_b = o_z + gdn_width
    o_a = o_b + n_heads
    o_lx = o_a + n_heads
    o_lg = o_lx + lru_width
    wconv = jnp.concatenate([w_in[:, o_qkv:o_z], w_in[:, o_lx:o_lg]], axis=1).astype(_BF16)
    wgate = jnp.concatenate([w_in[:, o_z:o_b], w_in[:, o_lg:o_lg + lru_width]], axis=1).astype(_BF16)
    wba = jnp.pad(w_in[:, o_b:o_lx], ((0, 0), (0, LANES - 2 * n_heads))).astype(_BF16)
    cw = jnp.concatenate([gdn_conv_w, lru_conv_w], axis=1).astype(_F32)
    pad_heads = lambda p: jnp.pad(p.astype(_F32), (n_heads, LANES - 2 * n_heads)).reshape(1, LANES)
    wlru = jnp.concatenate([_block_diag(lru_gate_a_w), _block_diag(lru_gate_x_w)], axis=1).astype(_BF16)
    blru = jnp.concatenate([lru_gate_a_b.reshape(1, -1), lru_gate_x_b.reshape(1, -1)], axis=1)

    n_tok = bsz * seq
    x2d = x.reshape(n_tok, d_model)
    gdn, lru = _mixer(
        x2d, norm_mix_w.reshape(1, -1), wconv, wgate, wba, cw, lru_conv_b.reshape(1, -1),
        pad_heads(gdn_A_log), pad_heads(gdn_dt_bias), gdn_norm_w.reshape(1, -1),
        wlru, blru, lru_a_param.reshape(1, -1), ts=seq_tile, seq=seq)

    out = _mlp(x2d, gdn, lru,
               w_out[:gdn_width].astype(_BF16), w_out[gdn_width:].astype(_BF16),
               norm_mlp_w.reshape(1, -1), w_ff1.astype(_BF16), w_ff2.astype(_BF16),
               final_norm_w.reshape(1, -1), tm=tok_tile, ff_chunk=ff_chunk)
    return out.reshape(bsz, seq, d_model)


def kernel(x, norm_mix_w, w_in, gdn_conv_w, gdn_A_log, gdn_dt_bias, gdn_norm_w, lru_conv_w, lru_conv_b, lru_gate_a_w, lru_gate_a_b, lru_gate_x_w, lru_gate_x_b, lru_a_param, w_out, norm_mlp_w, w_ff1, w_ff2, final_norm_w):
    assert norm_mix_w.shape[0] == 1, "single-layer stack"
    return _layer(x, norm_mix_w[0], w_in[0], gdn_conv_w[0], gdn_A_log[0], gdn_dt_bias[0],
                  gdn_norm_w[0], lru_conv_w[0], lru_conv_b[0], lru_gate_a_w[0], lru_gate_a_b[0],
                  lru_gate_x_w[0], lru_gate_x_b[0], lru_a_param[0], w_out[0], norm_mlp_w[0],
                  w_ff1[0], w_ff2[0], final_norm_w, seq_tile=512, tok_tile=1024, ff_chunk=2048)
```
